```python
import math
import jax, jax.numpy as jnp
from jax import lax
import numpy as np

D_MODEL = 2048
BATCH = 2
SEQ = 4096
DEPTH = 2

GRID_W = 64
CTX_LEN = 256
D_MIX = D_MODEL
W_GRP = D_MIX // 4
EPS = 1e-6
CHUNK = 128
HD_A = 128
H_A = W_GRP // HD_A
HD_V = 128
H_B = W_GRP // HD_V
HD_QK = HD_V // 2
ROPE_BASE = 10000.0
Q_BLOCK = 128
HD_C = 64
H_C = W_GRP // HD_C
LORA_W = 64
LORA_A = 32
RWKV_F = 3 * W_GRP + LORA_W + LORA_A
GN_EPS = 64e-5
H_D = 4
CONV_W = 3

PROJ_SIZES = (W_GRP, W_GRP, W_GRP,
              W_GRP, W_GRP, W_GRP, W_GRP,
              3 * W_GRP, LORA_W + LORA_A, LORA_W + LORA_A, W_GRP,
              W_GRP, W_GRP, W_GRP, W_GRP)
N_IN = sum(PROJ_SIZES)

kernel_name = 'hybrid_dit_parallel_groups_ctx_prefix'


def _f32(a):
    return a.astype(jnp.float32)


def rmsnorm(x, g):
    xf = _f32(x)
    y = xf * lax.rsqrt(jnp.mean(xf * xf, axis=-1, keepdims=True) + EPS)
    return (y * _f32(g)).astype(x.dtype)


def split_proj(p):
    idx = np.cumsum(PROJ_SIZES)[:-1].tolist()
    return jnp.split(p, idx, axis=-1)


def rope_1d(x, pos):
    d = x.shape[-1]
    inv = ROPE_BASE ** (-jnp.arange(0, d, 2, dtype=jnp.float32) / d)
    ang = pos.astype(jnp.float32)[:, None] * inv[None, :]
    cos = jnp.concatenate([jnp.cos(ang), jnp.cos(ang)], -1)[:, None, None, :]
    sin = jnp.concatenate([jnp.sin(ang), jnp.sin(ang)], -1)[:, None, None, :]
    x1, x2 = jnp.split(x, 2, axis=-1)
    rot = jnp.concatenate([-x2, x1], axis=-1)
    return (_f32(x) * cos + _f32(rot) * sin).astype(x.dtype)


def rope_2d(x, row, col):
    half = x.shape[-1] // 2
    return jnp.concatenate([rope_1d(x[..., :half], row), rope_1d(x[..., half:], col)], axis=-1)


def chunk_sgu(u, v, w_s, b_s):
    b, t, _ = u.shape
    n = t // CHUNK
    vf = _f32(v).reshape(b, n, CHUNK, H_A, HD_A)
    mean = jnp.mean(vf, -1, keepdims=True)
    var = jnp.mean(jnp.square(vf - mean), -1, keepdims=True)
    vn = (vf - mean) * lax.rsqrt(var + EPS)
    mixed = jnp.einsum('hpq,bnqhd->bnphd', _f32(w_s), vn) + _f32(b_s).T[None, None, :, :, None]
    return u * mixed.reshape(b, t, W_GRP).astype(u.dtype)


def diff_attn_core(q, k, v, lam):
    s = _f32(jnp.einsum('bqhmd,bkhmd->bhmqk', q, k)) * (HD_QK ** -0.5)
    p = jax.nn.softmax(s, axis=-1)
    a = p[:, :, 0] - lam * p[:, :, 1]
    return jnp.einsum('bhqk,bkhd->bqhd', a.astype(v.dtype), v)


def diff_attention(qc, kc, vc, ql, kl, vl, lam, lam_init, subln_g, need_ctx):
    b, n = ql.shape[:2]
    k_all = jnp.concatenate([kc, kl], axis=1)
    v_all = jnp.concatenate([vc, vl], axis=1)
    nb = n // Q_BLOCK
    qb = ql.reshape(b, nb, Q_BLOCK, H_B, 2, HD_QK).transpose(1, 0, 2, 3, 4, 5)
    ob = lax.map(lambda q: diff_attn_core(q, k_all, v_all, lam), qb)
    o_lat = ob.transpose(1, 0, 2, 3, 4).reshape(b, n, H_B, HD_V)

    def post(o):
        return (rmsnorm(o, subln_g) * (1.0 - lam_init)).reshape(o.shape[0], o.shape[1], W_GRP)

    o_ctx = post(diff_attn_core(qc, kc, vc, lam)) if need_ctx else None
    return o_ctx, post(o_lat)


def rwkv_scan(r, k, v, decay, kk, a, s0):
    def step(S, inp):
        r_t, k_t, v_t, w_t, kk_t, a_t = inp
        sa = jnp.einsum('bhvk,bhk->bhv', S, -kk_t)
        S = (S * w_t[:, :, None, :] + sa[..., None] * (kk_t * a_t)[:, :, None, :]
             + v_t[..., None] * k_t[:, :, None, :])
        y = jnp.einsum('bhvk,bhk->bhv', S, r_t)
        return S, y

    xs = tuple(jnp.moveaxis(z, 1, 0) for z in (r, k, v, decay, kk, a))
    s_final, ys = lax.scan(step, s0, xs)
    return jnp.moveaxis(ys, 0, 1), s_final


def rwkv_direction(feats, s0, mu, w0, w2, a0, a2, k_k, k_a, r_k):
    b, t, _ = feats.shape
    f = _f32(feats)
    prev = jnp.pad(f, ((0, 0), (1, 0), (0, 0)))[:, :t]
    z = f + (prev - f) * _f32(mu)
    r, k, v, zw, za = jnp.split(z, [W_GRP, 2 * W_GRP, 3 * W_GRP, 3 * W_GRP + LORA_W], axis=-1)
    w = -jax.nn.softplus(-(_f32(w0) + jnp.tanh(zw) @ _f32(w2))) - 0.5
    decay = jnp.exp(-jnp.exp(w))
    a = jax.nn.sigmoid(_f32(a0) + za @ _f32(a2))
    heads = lambda y: y.reshape(b, t, H_C, HD_C)
    kk = heads(k * _f32(k_k))
    kk = kk / jnp.maximum(jnp.sqrt(jnp.sum(kk * kk, -1, keepdims=True)), 1e-12)
    k = k * (1.0 + (a - 1.0) * _f32(k_a))
    r, k, v, decay, a = heads(r), heads(k), heads(v), heads(decay), heads(a)
    y, s_final = rwkv_scan(r, k, v, decay, kk, a, s0)
    bonus = jnp.sum(r * k * _f32(r_k), -1, keepdims=True) * v
    return y, bonus, s_final


def rwkv_mixer(rkv_c, lf_c, lb_c, rkv_l, lf_l, lb_l, p_fwd, p_bwd, ln_w, ln_b, need_ctx):
    s0 = jnp.zeros((rkv_l.shape[0], H_C, HD_C, HD_C), jnp.float32)
    flip = lambda y: jnp.flip(y, axis=1)
    cat = lambda p, q: jnp.concatenate([p, q], axis=-1)
    yc_f, bc_f, sc_f = rwkv_direction(cat(rkv_c, lf_c), s0, *p_fwd)
    yl_f, bl_f, _ = rwkv_direction(cat(rkv_l, lf_l), sc_f, *p_fwd)
    yc_b, bc_b, sc_b = rwkv_direction(flip(cat(rkv_c, lb_c)), s0, *p_bwd)
    yl_b, bl_b, _ = rwkv_direction(flip(cat(rkv_l, lb_l)), sc_b, *p_bwd)

    def merge(yf, bf, yb, bb):
        y = yf + flip(yb)
        mean = jnp.mean(y, -1, keepdims=True)
        var = jnp.mean(jnp.square(y - mean), -1, keepdims=True)
        yn = ((y - mean) * lax.rsqrt(var + GN_EPS)).reshape(y.shape[0], y.shape[1], W_GRP)
        yn = yn * _f32(ln_w) + _f32(ln_b)
        return (yn + (bf + flip(bb)).reshape(yn.shape)).astype(rkv_l.dtype)

    o_ctx = merge(yc_f, bc_f, yc_b, bc_b) if need_ctx else None
    return o_ctx, merge(yl_f, bl_f, yl_b, bl_b)


def short_conv(bg, cg, xin, w):
    z = cg * xin
    y = lax.conv_general_dilated(z, w[:, None, :].astype(z.dtype), window_strides=(1,),
                                 padding=[(CONV_W // 2, CONV_W // 2)],
                                 dimension_numbers=('NWC', 'WIO', 'NWC'),
                                 feature_group_count=W_GRP)
    return bg * y


def setup_inputs(seed: int = 0) -> dict:
    key = jax.random.key(seed)
    ks = jax.random.split(key, 32)
    nrm = lambda k, s: jax.random.normal(k, s, jnp.float32)
    return {
        'x': nrm(ks[0], (BATCH, SEQ, D_MODEL)),
        'c': nrm(ks[1], (BATCH, D_MODEL)),
        'ctx': nrm(ks[2], (BATCH, CTX_LEN, D_MODEL)),
        'c_ctx': nrm(ks[3], (D_MODEL,)),
        'w_mod': nrm(ks[4], (DEPTH, D_MODEL, 3 * D_MODEL)) * (0.5 * D_MODEL ** -0.5),
        'b_mod': nrm(ks[5], (DEPTH, 3 * D_MODEL)) * 0.02,
        'g_pre': 1.0 + 0.02 * nrm(ks[6], (DEPTH, D_MODEL)),
        'g_post': 1.0 + 0.02 * nrm(ks[7], (DEPTH, D_MODEL)),
        'w_in': nrm(ks[8], (DEPTH, D_MODEL, N_IN)) * D_MODEL ** -0.5,
        'w_out': nrm(ks[9], (DEPTH, D_MIX, D_MODEL)) * D_MIX ** -0.5,
        'sgu_w': nrm(ks[10], (DEPTH, H_A, CHUNK, CHUNK)) * CHUNK ** -0.5,
        'sgu_b': 1.0 + 0.1 * nrm(ks[11], (DEPTH, H_A, CHUNK)),
        'lam_q1': 0.1 * nrm(ks[12], (DEPTH, HD_QK)),
        'lam_k1': 0.1 * nrm(ks[13], (DEPTH, HD_QK)),
        'lam_q2': 0.1 * nrm(ks[14], (DEPTH, HD_QK)),
        'lam_k2': 0.1 * nrm(ks[15], (DEPTH, HD_QK)),
        'subln_g': 1.0 + 0.02 * nrm(ks[16], (DEPTH, HD_V)),
        'rwkv_mu': jax.random.uniform(ks[17], (DEPTH, 2, RWKV_F), jnp.float32),
        'rwkv_w0': jax.random.uniform(ks[18], (DEPTH, 2, W_GRP), jnp.float32, -4.0, -0.5),
        'rwkv_w2': nrm(ks[19], (DEPTH, 2, LORA_W, W_GRP)) * (0.5 * LORA_W ** -0.5),
        'rwkv_a0': 0.5 * nrm(ks[20], (DEPTH, 2, W_GRP)),
        'rwkv_a2': nrm(ks[21], (DEPTH, 2, LORA_A, W_GRP)) * (0.5 * LORA_A ** -0.5),
        'rwkv_kk': 0.85 + 0.05 * nrm(ks[22], (DEPTH, 2, W_GRP)),
        'rwkv_ka': 1.0 + 0.05 * nrm(ks[23], (DEPTH, 2, W_GRP)),
        'rwkv_rk': 0.1 * nrm(ks[24], (DEPTH, 2, H_C, HD_C)),
        'rwkv_ln_w': 1.0 + 0.02 * nrm(ks[25], (DEPTH, W_GRP)),
        'rwkv_ln_b': 0.02 * nrm(ks[26], (DEPTH, W_GRP)),
        'conv_w': nrm(ks[27], (DEPTH, CONV_W, W_GRP)) * CONV_W ** -0.5,
    }


def reference(x, c, ctx, c_ctx, w_mod, b_mod, g_pre, g_post, w_in, w_out, sgu_w, sgu_b,
              lam_q1, lam_k1, lam_q2, lam_k2, subln_g, rwkv_mu, rwkv_w0, rwkv_w2, rwkv_a0,
              rwkv_a2, rwkv_kk, rwkv_ka, rwkv_rk, rwkv_ln_w, rwkv_ln_b, conv_w):
    b, n, _ = x.shape
    rows = n // GRID_W
    row = jnp.repeat(jnp.arange(rows), GRID_W)
    col = jnp.tile(jnp.arange(GRID_W), rows)
    s_c = jax.nn.silu(c)
    s_cc = jax.nn.silu(c_ctx)
    xc = ctx
    for l in range(DEPTH):
        need_ctx = l < DEPTH - 1
        shift, scale, gate = jnp.split((s_c @ w_mod[l] + b_mod[l])[:, None, :], 3, axis=-1)
        shift_c, scale_c, gate_c = jnp.split(s_cc @ w_mod[l] + b_mod[l], 3, axis=-1)
        hl = rmsnorm(x, g_pre[l]) * (1.0 + scale) + shift
        hc = rmsnorm(xc, g_pre[l]) * (1.0 + scale_c) + shift_c
        pl = split_proj(hl @ w_in[l])
        pc = split_proj(hc @ w_in[l])

        oA_l = chunk_sgu(pl[0], pl[1], sgu_w[l], sgu_b[l]) * jax.nn.silu(pl[2])
        oA_c = chunk_sgu(pc[0], pc[1], sgu_w[l], sgu_b[l]) * jax.nn.silu(pc[2]) if need_ctx else None

        ql = rope_2d(pl[3].reshape(b, n, H_B, 2, HD_QK), row, col)
        kl = rope_2d(pl[4].reshape(b, n, H_B, 2, HD_QK), row, col)
        vl = pl[5].reshape(b, n, H_B, HD_V)
        qc = pc[3].reshape(b, CTX_LEN, H_B, 2, HD_QK)
        kc = pc[4].reshape(b, CTX_LEN, H_B, 2, HD_QK)
        vc = pc[5].reshape(b, CTX_LEN, H_B, HD_V)
        lam_init = 0.8 - 0.6 * math.exp(-0.3 * l)
        lam = (jnp.exp(jnp.sum(_f32(lam_q1[l]) * _f32(lam_k1[l])))
               - jnp.exp(jnp.sum(_f32(lam_q2[l]) * _f32(lam_k2[l]))) + lam_init)
        oB_c, oB_l = diff_attention(qc, kc, vc, ql, kl, vl, lam, lam_init, subln_g[l], need_ctx)
        oB_l = oB_l * jax.nn.silu(pl[6])
        oB_c = oB_c * jax.nn.silu(pc[6]) if need_ctx else None

        p_fwd = (rwkv_mu[l, 0], rwkv_w0[l, 0], rwkv_w2[l, 0], rwkv_a0[l, 0], rwkv_a2[l, 0],
                 rwkv_kk[l, 0], rwkv_ka[l, 0], rwkv_rk[l, 0])
        p_bwd = (rwkv_mu[l, 1], rwkv_w0[l, 1], rwkv_w2[l, 1], rwkv_a0[l, 1], rwkv_a2[l, 1],
                 rwkv_kk[l, 1], rwkv_ka[l, 1], rwkv_rk[l, 1])
        oC_c, oC_l = rwkv_mixer(pc[7], pc[8], pc[9], pl[7], pl[8], pl[9], p_fwd, p_bwd,
                                rwkv_ln_w[l], rwkv_ln_b[l], need_ctx)
        oC_l = oC_l * jax.nn.silu(pl[10])
        oC_c = oC_c * jax.nn.silu(pc[10]) if need_ctx else None

        oD_l = short_conv(pl[11], pl[12], pl[13], conv_w[l]) * jax.nn.silu(pl[14])
        oD_c = short_conv(pc[11], pc[12], pc[13], conv_w[l]) * jax.nn.silu(pc[14]) if need_ctx else None

        o_l = jnp.concatenate([oA_l, oB_l, oC_l, oD_l], axis=-1) @ w_out[l]
        x = x + gate * rmsnorm(o_l, g_post[l])
        if need_ctx:
            o_c = jnp.concatenate([oA_c, oB_c, oC_c, oD_c], axis=-1) @ w_out[l]
            xc = xc + gate_c * rmsnorm(o_c, g_post[l])
    return x
```

```python
import functools
import math

import jax
import jax.numpy as jnp
from jax import lax
from jax.experimental import pallas as pl
from jax.experimental.pallas import tpu as pltpu

F32 = jnp.float32
BF16 = jnp.bfloat16

EPS = 1e-6
GN_EPS = 64e-5
GRID_W = 64
ROPE_BASE = 10000.0
LOG2E = 1.4426950408889634

W_GRP = 512
CHUNK_A = 128
HD_B = 128
HD_QK = 64
HD_C = 64
LORA_W = 64
LORA_A = 32
LORA_PAD = 128
RW_L = 64
RW_G = 256
TM = 256
VMEM_LIMIT = 56 * 1024 * 1024

N_A = 3 * W_GRP
N_B = 4 * W_GRP
N_D = 4 * W_GRP
N_C = 4 * W_GRP + 2 * LORA_PAD
OFF_A, OFF_B, OFF_D, OFF_C = 0, N_A, N_A + N_B, N_A + N_B + N_D
N_ALL = OFF_C + N_C


def _silu(g):
    return g * (1.0 / (1.0 + jnp.exp(-g)))


def _sigmoid(x):
    return 1.0 / (1.0 + jnp.exp(-x))


def _split_bf16(x):
    hi = x.astype(BF16)
    lo = (x - hi.astype(F32)).astype(BF16)
    return hi, lo


def _mm(a, b):
    return jnp.dot(a, b, preferred_element_type=F32)


def _mm_nt(a, b):
    return lax.dot_general(a, b, (((1,), (1,)), ((), ())), preferred_element_type=F32)


def _params(sem):
    return pltpu.CompilerParams(dimension_semantics=sem, vmem_limit_bytes=VMEM_LIMIT)


def _mod_kernel(s_ref, w_ref, b_ref, o_ref):
    s = _silu(s_ref[...])
    o_ref[0] = _mm(s.astype(BF16), w_ref[0].astype(BF16)) + b_ref[0]


def _modulation(c, c_ctx, w_mod, b_mod):
    depth, d, n3 = w_mod.shape
    b = c.shape[0]
    rows = jnp.zeros((8, d), F32).at[:b].set(c).at[b].set(c_ctx)
    tn = 768
    return pl.pallas_call(
        _mod_kernel,
        grid=(depth, n3 // tn),
        in_specs=[pl.BlockSpec((8, d), lambda l, j: (0, 0)),
                  pl.BlockSpec((1, d, tn), lambda l, j: (l, 0, j)),
                  pl.BlockSpec((1, 1, tn), lambda l, j: (l, 0, j))],
        out_specs=pl.BlockSpec((1, 8, tn), lambda l, j: (l, 0, j)),
        out_shape=jax.ShapeDtypeStruct((depth, 8, n3), F32),
        compiler_params=_params(("parallel", "parallel")),
        name="modulation",
    )(rows, w_mod, b_mod.reshape(depth, 1, n3))


def _inproj_kernel(n_batch, xc_ref, xl_ref, mod_ref, g_ref, w_ref, cos_ref, sa_ref, sb_ref,
                   pa_ref, pb_ref, pd_ref, pc_ref):
    b = pl.program_id(0)
    i = pl.program_id(1)
    d = xc_ref.shape[-1]
    is_ctx = i == 0
    x = jnp.where(is_ctx, xc_ref[0], xl_ref[0])
    y = x * lax.rsqrt(jnp.mean(x * x, axis=-1, keepdims=True) + EPS) * g_ref[...]
    row = jnp.where(is_ctx, mod_ref[0, pl.ds(n_batch, 1), :], mod_ref[0, pl.ds(b, 1), :])
    h = (y * (1.0 + row[:, d:2 * d]) + row[:, :d]).astype(BF16)

    def proj(c0, c1):
        return _mm(h, w_ref[:, c0:c1])

    pa_ref[0] = proj(OFF_A, OFF_A + N_A).astype(BF16)
    cos, sa, sb = cos_ref[...], sa_ref[...], sb_ref[...]
    qk = proj(OFF_B, OFF_B + 2 * W_GRP)
    for j in range(2 * W_GRP // HD_B):
        t = qk[:, j * HD_B:(j + 1) * HD_B]
        t = t * cos + pltpu.roll(t, HD_B - 16, axis=1) * sa + pltpu.roll(t, 16, axis=1) * sb
        if j < W_GRP // HD_B:
            t = t * (HD_QK ** -0.5 * LOG2E)
        pb_ref[0, :, j * HD_B:(j + 1) * HD_B] = t.astype(BF16)
    pb_ref[0, :, 2 * W_GRP:] = proj(OFF_B + 2 * W_GRP, OFF_B + N_B).astype(BF16)
    pd_ref[0] = proj(OFF_D, OFF_D + N_D).astype(BF16)
    pc_ref[0] = proj(OFF_C, OFF_C + N_C).astype(BF16)


def _inproj(xc, xl, mods, layer, g_pre, w_in, rope):
    b, n_ctx, d = xc.shape
    n_lat = xl.shape[1]
    t = n_ctx + n_lat
    assert n_ctx == TM and n_lat % TM == 0
    nblk = t // TM
    cos, sa, sb = rope
    tab = pl.BlockSpec((TM, HD_B), lambda bb, i: (i, 0))
    out = lambda n: pl.BlockSpec((1, TM, n), lambda bb, i: (bb, i, 0))
    return pl.pallas_call(
        functools.partial(_inproj_kernel, b),
        grid=(b, nblk),
        in_specs=[pl.BlockSpec((1, TM, d), lambda bb, i: (bb, 0, 0)),
                  pl.BlockSpec((1, TM, d), lambda bb, i: (bb, jnp.maximum(i - 1, 0), 0)),
                  pl.BlockSpec((1, 8, 3 * d), lambda bb, i: (layer, 0, 0)),
                  pl.BlockSpec((1, d), lambda bb, i: (0, 0)),
                  pl.BlockSpec((d, N_ALL), lambda bb, i: (0, 0), pipeline_mode=pl.Buffered(1)),
                  tab, tab, tab],
        out_specs=[out(N_A), out(N_B), out(N_D), out(N_C)],
        out_shape=[jax.ShapeDtypeStruct((b, t, n), BF16) for n in (N_A, N_B, N_D, N_C)],
        compiler_params=_params(("parallel", "arbitrary")),
        name="inproj",
    )(xc, xl, mods, g_pre.reshape(1, d), w_in, cos, sa, sb)


def _local_kernel(n_blk, pa_ref, pd_ref, hp_ref, hn_ref, ws_ref, bs_ref, cw_ref, oa_ref, od_ref):
    i = pl.program_id(1)
    for c in range(TM // CHUNK_A):
        rows = slice(c * CHUNK_A, (c + 1) * CHUNK_A)
        for h in range(W_GRP // CHUNK_A):
            cols = slice(h * CHUNK_A, (h + 1) * CHUNK_A)
            u = pa_ref[0, rows, cols].astype(F32)
            v = pa_ref[0, rows, W_GRP + h * CHUNK_A:W_GRP + (h + 1) * CHUNK_A].astype(F32)
            g = pa_ref[0, rows, 2 * W_GRP + h * CHUNK_A:2 * W_GRP + (h + 1) * CHUNK_A].astype(F32)
            mean = jnp.mean(v, axis=-1, keepdims=True)
            vc = v - mean
            vn = vc * lax.rsqrt(jnp.mean(vc * vc, axis=-1, keepdims=True) + EPS)
            mixed = _mm(ws_ref[h].astype(BF16), vn.astype(BF16)) + bs_ref[:, h:h + 1]
            oa_ref[0, rows, cols] = (u * mixed * _silu(g)).astype(BF16)
    cx = pd_ref[0, :, 0:2 * W_GRP].astype(F32)
    z = cx[:, :W_GRP] * cx[:, W_GRP:]
    hp = hp_ref[0].astype(F32)
    hn = hn_ref[0].astype(F32)
    zp = hp[15:16, :W_GRP] * hp[15:16, W_GRP:]
    zn = hn[0:1, :W_GRP] * hn[0:1, W_GRP:]
    zp = jnp.where(i <= 1, 0.0, zp)
    zn = jnp.where(jnp.logical_or(i == 0, i == n_blk - 1), 0.0, zn)
    r = lax.broadcasted_iota(jnp.int32, (TM, 1), 0)
    z_prev = jnp.where(r == 0, zp, pltpu.roll(z, 1, axis=0))
    z_next = jnp.where(r == TM - 1, zn, pltpu.roll(z, TM - 1, axis=0))
    w = cw_ref[...]
    y = z_prev * w[0:1] + z * w[1:2] + z_next * w[2:3]
    bg = pd_ref[0, :, 2 * W_GRP:3 * W_GRP].astype(F32)
    g = pd_ref[0, :, 3 * W_GRP:].astype(F32)
    od_ref[0] = (bg * y * _silu(g)).astype(BF16)


def _local_mixers(pa, pd, sgu_w, sgu_b, conv_w):
    b, t, _ = pa.shape
    nblk = t // TM
    hb = TM // 16
    return pl.pallas_call(
        functools.partial(_local_kernel, nblk),
        grid=(b, nblk),
        in_specs=[pl.BlockSpec((1, TM, N_A), lambda bb, i: (bb, i, 0)),
                  pl.BlockSpec((1, TM, N_D), lambda bb, i: (bb, i, 0)),
                  pl.BlockSpec((1, 16, 2 * W_GRP), lambda bb, i: (bb, jnp.maximum(i * hb - 1, 0), 0)),
                  pl.BlockSpec((1, 16, 2 * W_GRP),
                               lambda bb, i: (bb, jnp.minimum((i + 1) * hb, nblk * hb - 1), 0)),
                  pl.BlockSpec(sgu_w.shape, lambda bb, i: (0, 0, 0)),
                  pl.BlockSpec((CHUNK_A, sgu_w.shape[0]), lambda bb, i: (0, 0)),
                  pl.BlockSpec(conv_w.shape, lambda bb, i: (0, 0))],
        out_specs=[pl.BlockSpec((1, TM, W_GRP), lambda bb, i: (bb, i, 0))] * 2,
        out_shape=[jax.ShapeDtypeStruct((b, t, W_GRP), BF16)] * 2,
        compiler_params=_params(("parallel", "parallel")),
        name="local_mixers",
    )(pa, pd, pd, pd, sgu_w, sgu_b.T, conv_w)


def _attn_kernel(lam_init, ctx_first, q_ref, k_ref, v_ref, g_ref, lam_ref, sg_ref, o_ref, vext_ref):
    tq = q_ref.shape[1]
    qi = pl.program_id(2)

    @pl.when(qi == 0)
    def _():
        vext_ref[:, :HD_B] = v_ref[0]
        vext_ref[:, HD_B:] = jnp.ones((vext_ref.shape[0], HD_B), BF16)

    lp = lam_ref[...]
    lam = (jnp.exp(jnp.sum(lp[0:1] * lp[1:2], axis=-1, keepdims=True))
           - jnp.exp(jnp.sum(lp[2:3] * lp[3:4], axis=-1, keepdims=True)) + lam_init)
    q = q_ref[0]
    lane = lax.broadcasted_iota(jnp.int32, (1, HD_B), 1)
    zero = jnp.zeros_like(q)
    q2 = jnp.concatenate([jnp.where(lane < HD_QK, q, zero), jnp.where(lane >= HD_QK, q, zero)], axis=0)

    def attend(n_keys):
        s = _mm_nt(q2, k_ref[0, :n_keys])
        p = jnp.exp2(s - jnp.max(s, axis=-1, keepdims=True)).astype(BF16)
        o = _mm(p, vext_ref[:n_keys])
        a = o[:tq, :HD_B] / o[:tq, HD_B:] - lam * (o[tq:, :HD_B] / o[tq:, HD_B:])
        y = a * lax.rsqrt(jnp.mean(a * a, axis=-1, keepdims=True) + EPS) * sg_ref[...] * (1.0 - lam_init)
        o_ref[0] = (y * _silu(g_ref[0].astype(F32))).astype(BF16)

    if ctx_first:
        pl.when(qi == 0)(lambda: attend(tq))
        pl.when(qi != 0)(lambda: attend(k_ref.shape[1]))
    else:
        attend(k_ref.shape[1])


def _attention(pb, lam_params, subln_g, lam_init, q_blk0):
    b, t, _ = pb.shape
    nh = W_GRP // HD_B
    nq = t // TM - q_blk0
    return pl.pallas_call(
        functools.partial(_attn_kernel, lam_init, q_blk0 == 0),
        grid=(b, nh, nq),
        in_specs=[pl.BlockSpec((1, TM, HD_B), lambda bb, h, i: (bb, i + q_blk0, h)),
                  pl.BlockSpec((1, t, HD_B), lambda bb, h, i: (bb, 0, nh + h)),
                  pl.BlockSpec((1, t, HD_B), lambda bb, h, i: (bb, 0, 2 * nh + h)),
                  pl.BlockSpec((1, TM, HD_B), lambda bb, h, i: (bb, i + q_blk0, 3 * nh + h)),
                  pl.BlockSpec((4, HD_QK), lambda bb, h, i: (0, 0)),
                  pl.BlockSpec((1, HD_B), lambda bb, h, i: (0, 0))],
        out_specs=pl.BlockSpec((1, TM, HD_B), lambda bb, h, i: (bb, i + q_blk0, h)),
        out_shape=jax.ShapeDtypeStruct((b, t, W_GRP), BF16),
        scratch_shapes=[pltpu.VMEM((t, 2 * HD_B), BF16)],
        compiler_params=_params(("parallel", "parallel", "arbitrary")),
        name="diff_attention",
    )(pb, pb, pb, pb, lam_params, subln_g.reshape(1, HD_B))


def _seg_sum(x, ones_bd):
    hi, lo = _split_bf16(x)
    return _mm(hi, ones_bd) + _mm(lo, ones_bd)


def _head_diag(x4, mask):
    return jnp.where(mask, x4, 0.0).astype(BF16)


def _rwkv_prep_kernel(n_blk, rkv_ref, lo_ref, hrkv_ref, hlo_ref, mu_ref, mul_ref, w0_ref, w2_ref,
                      a0_ref, a2_ref, kk_ref, ka_ref, rk_ref,
                      g_ref, q_ref, rp_ref, yp_ref, bonus_ref):
    dr = pl.program_id(1)
    i = pl.program_id(2)
    fwd = dr == 0
    nl = RW_G // HD_C
    sgn = 1 - 2 * dr
    hrow = jnp.where(fwd, 15, 0)
    edge = jnp.logical_or(i == 0, i == jnp.where(fwd, 1, n_blk - 1))
    rsel = lax.broadcasted_iota(jnp.int32, (TM, 1), 0) == jnp.where(fwd, 0, TM - 1)
    hsel = lax.broadcasted_iota(jnp.int32, (16, 1), 0) == hrow

    def shifted(cur_ref, halo_ref, mu):
        f = cur_ref[0].astype(F32)
        halo = jnp.sum(jnp.where(hsel, halo_ref[0].astype(F32), 0.0), axis=0, keepdims=True)
        halo = jnp.where(edge, 0.0, halo)
        rolled = jnp.where(fwd, pltpu.roll(f, 1, axis=0), pltpu.roll(f, TM - 1, axis=0))
        prev = jnp.where(rsel, halo, rolled)
        return f + (prev - f) * mu

    z = shifted(rkv_ref, hrkv_ref, mu_ref[0])
    zl = shifted(lo_ref, hlo_ref, mul_ref[0])
    r, k, v = z[:, :W_GRP], z[:, W_GRP:2 * W_GRP], z[:, 2 * W_GRP:]
    wl = w0_ref[0] + _mm(jnp.tanh(zl).astype(BF16), w2_ref[0].astype(BF16))
    logd = -math.exp(-0.5) * _sigmoid(wl)
    a = _sigmoid(a0_ref[0] + _mm(zl.astype(BF16), a2_ref[0].astype(BF16)))
    ri = lax.broadcasted_iota(jnp.int32, (W_GRP, W_GRP), 0) // HD_C
    ci = lax.broadcasted_iota(jnp.int32, (W_GRP, W_GRP), 1) // HD_C
    ones_bd = jnp.where(ri == ci, 1.0, 0.0).astype(BF16)
    kk = k * kk_ref[0]
    kk = kk / jnp.maximum(jnp.sqrt(_seg_sum(kk * kk, ones_bd)), 1e-12)
    k2 = k * (1.0 + (a - 1.0) * ka_ref[0])
    bv = kk * a
    bonus_ref[0, 0] = _seg_sum(r * k2 * rk_ref[0], ones_bd) * v
    rt = lax.broadcasted_iota(jnp.int32, (TM, TM), 0)
    ct = lax.broadcasted_iota(jnp.int32, (TM, TM), 1)
    same = rt // RW_L == ct // RW_L
    tri = jnp.where(jnp.logical_and(same, (rt - ct) * sgn >= 0), 1.0, 0.0).astype(BF16)
    blk = jnp.where(same, 1.0, 0.0).astype(BF16)
    lh, ll = _split_bf16(logd)
    cum = _mm(tri, lh) + _mm(tri, ll)
    tot = _mm(blk, lh) + _mm(blk, ll)
    half = 0.5 * tot
    e_pos = jnp.exp(cum - half)
    e_neg = jnp.exp(half - cum)
    e_prev = jnp.exp(cum - logd - half)
    e_end = jnp.exp(tot - cum)
    g_end = jnp.exp(tot)
    e_half = jnp.exp(half)
    kt_all = kk * e_prev
    rt_all = r * e_pos
    kh_all = k2 * e_neg
    bh_all = bv * e_neg
    kb_all = k2 * e_end
    bb_all = bv * e_end
    row = lax.broadcasted_iota(jnp.int32, (RW_L, RW_G), 0)
    col = lax.broadcasted_iota(jnp.int32, (RW_L, RW_G), 1) % HD_C
    eye = jnp.where(col == row, 1.0, 0.0)
    r4 = lax.broadcasted_iota(jnp.int32, (RW_G, RW_G), 0) // HD_C
    c4 = lax.broadcasted_iota(jnp.int32, (RW_G, RW_G), 1) // HD_C
    dmask = r4 == c4
    r2 = lax.broadcasted_iota(jnp.int32, (2 * RW_L, RW_G), 0) % RW_L
    c2 = lax.broadcasted_iota(jnp.int32, (2 * RW_L, RW_G), 1) % HD_C
    first = jnp.where(lax.broadcasted_iota(jnp.int32, (2 * RW_L, RW_G), 0) < RW_L, 1, 0)
    mask2 = (r2 - c2) * sgn >= first

    def bd(x):
        return _head_diag(jnp.concatenate([x] * nl, axis=0), dmask)

    def hprod(x, y):
        return _mm(x.astype(BF16), bd(y))

    def diag_blocks(full):
        out = jnp.zeros((RW_L, RW_G), F32)
        lane_h = lax.broadcasted_iota(jnp.int32, (RW_L, RW_G), 1) // HD_C
        for hh in range(nl):
            out = out + jnp.where(lane_h == hh, full[hh * HD_C:(hh + 1) * HD_C, :], 0.0)
        return out

    for c in range(TM // RW_L):
        rows = slice(c * RW_L, (c + 1) * RW_L)
        for gi in range(W_GRP // RW_G):
            cols = slice(gi * RW_G, (gi + 1) * RW_G)
            kt, rtl, kh, bh = kt_all[rows, cols], rt_all[rows, cols], kh_all[rows, cols], bh_all[rows, cols]
            kb, bb, vv = kb_all[rows, cols], bb_all[rows, cols], v[rows, cols]
            lhs = jnp.concatenate([kt, rtl], axis=0).astype(BF16)
            pb_ = jnp.where(mask2, _mm_nt(lhs, bd(bh)), 0.0)
            pk_ = jnp.where(mask2, _mm_nt(lhs, bd(kh)), 0.0)
            a_b, m_rb = pb_[:RW_L], pb_[RW_L:]
            tinv = eye - a_b
            pw = a_b
            for _ in range(5):
                pw = hprod(pw, pw)
                tinv = tinv + hprod(tinv, pw)
            wv = hprod(pk_, vv)
            tb = tinv.astype(BF16)
            wmat = _mm(tb, bd(kt))
            up = _mm(tb, bd(wv[:RW_L]))
            mb = m_rb.astype(BF16)
            eh = e_half[c * RW_L:c * RW_L + 1, cols]
            rp = (rtl - _mm(mb, bd(wmat))) * eh
            yp = wv[RW_L:] - _mm(mb, bd(up))
            xt = jnp.concatenate([kb, -bb], axis=0).T.astype(BF16)
            zeros = jnp.zeros((RW_L, RW_G), F32)
            yq = jnp.concatenate([vv, up], axis=0).astype(BF16)
            yg = jnp.concatenate([zeros, wmat], axis=0).astype(BF16)
            qm = diag_blocks(_mm(xt, yq))
            gm = diag_blocks(_mm(xt, yg)) * eh + eye * g_end[c * RW_L:c * RW_L + 1, cols]
            g_ref[0, 0, rows, cols] = gm
            q_ref[0, 0, rows, cols] = qm
            rp_ref[0, 0, rows, cols] = rp
            yp_ref[0, 0, rows, cols] = yp


def _rwkv_scan_kernel(n_blk, g_ref, q_ref, rp_ref, yp_ref, y_ref, h_ref):
    nl = RW_G // HD_C

    @pl.when(pl.program_id(2) == 0)
    def _():
        h_ref[...] = jnp.zeros_like(h_ref)

    dr = pl.program_id(1)
    r4 = lax.broadcasted_iota(jnp.int32, (RW_G, RW_G), 0) // HD_C
    c4 = lax.broadcasted_iota(jnp.int32, (RW_G, RW_G), 1) // HD_C
    dmask = r4 == c4
    n_chunk = TM // RW_L
    for s in range(n_chunk):
        c = jnp.where(dr == 0, s, n_chunk - 1 - s)
        rows = pl.ds(pl.multiple_of(c * RW_L, RW_L), RW_L)
        for gi in range(W_GRP // RW_G):
            cols = slice(gi * RW_G, (gi + 1) * RW_G)
            h = h_ref[:, cols]
            h4 = jnp.where(dmask, jnp.concatenate([h] * nl, axis=0), 0.0)
            hh, hl = _split_bf16(h4)
            lhs = jnp.concatenate([g_ref[0, 0, rows, cols], rp_ref[0, 0, rows, cols]], axis=0)
            lh, ll = _split_bf16(lhs)
            prod = _mm(lh, hh) + _mm(lh, hl) + _mm(ll, hh)
            y_ref[0, 0, rows, cols] = prod[RW_L:] + yp_ref[0, 0, rows, cols]
            h_ref[:, cols] = prod[:RW_L] + q_ref[0, 0, rows, cols]


def _rwkv_merge_kernel(y_ref, bonus_ref, g_ref, lw_ref, lb_ref, o_ref):
    ri = lax.broadcasted_iota(jnp.int32, (W_GRP, W_GRP), 0) // HD_C
    ci = lax.broadcasted_iota(jnp.int32, (W_GRP, W_GRP), 1) // HD_C
    ones_bd = jnp.where(ri == ci, 1.0, 0.0).astype(BF16)
    y = y_ref[0, 0] + y_ref[0, 1]
    mean = _seg_sum(y, ones_bd) * (1.0 / HD_C)
    yc = y - mean
    var = _seg_sum(yc * yc, ones_bd) * (1.0 / HD_C)
    yn = yc * lax.rsqrt(var + GN_EPS) * lw_ref[...] + lb_ref[...]
    out = yn + bonus_ref[0, 0] + bonus_ref[0, 1]
    o_ref[0] = (out * _silu(g_ref[0].astype(F32))).astype(BF16)


def _scan_block(dr, i, n_blk):
    return jnp.where(dr == 0, i, jnp.where(i == 0, 0, n_blk - i))


def _rwkv(pc, p):
    b, t, _ = pc.shape
    nblk = t // TM
    hb = TM // 16
    lblk = 4 * W_GRP // LORA_PAD

    def halo_idx(dr, i):
        return jnp.where(dr == 0, jnp.maximum(i * hb - 1, 0), jnp.minimum((i + 1) * hb, nblk * hb - 1))

    per_dir = lambda n: pl.BlockSpec((1, 1, n), lambda bb, dr, i: (dr, 0, 0))
    per_dir_m = lambda m, n: pl.BlockSpec((1, m, n), lambda bb, dr, i: (dr, 0, 0))
    state = lambda: pl.BlockSpec((1, 1, TM, W_GRP), lambda bb, dr, i: (bb, dr, i, 0))
    sds = jax.ShapeDtypeStruct((b, 2, t, W_GRP), F32)
    gm, qm, rp, yp, bonus = pl.pallas_call(
        functools.partial(_rwkv_prep_kernel, nblk),
        grid=(b, 2, nblk),
        in_specs=[pl.BlockSpec((1, TM, 3 * W_GRP), lambda bb, dr, i: (bb, i, 0)),
                  pl.BlockSpec((1, TM, LORA_PAD), lambda bb, dr, i: (bb, i, lblk + dr)),
                  pl.BlockSpec((1, 16, 3 * W_GRP), lambda bb, dr, i: (bb, halo_idx(dr, i), 0)),
                  pl.BlockSpec((1, 16, LORA_PAD), lambda bb, dr, i: (bb, halo_idx(dr, i), lblk + dr)),
                  per_dir(3 * W_GRP), per_dir(LORA_PAD), per_dir(W_GRP), per_dir_m(LORA_PAD, W_GRP),
                  per_dir(W_GRP), per_dir_m(LORA_PAD, W_GRP), per_dir(W_GRP), per_dir(W_GRP),
                  per_dir(W_GRP)],
        out_specs=[state()] * 5,
        out_shape=[sds] * 5,
        compiler_params=_params(("parallel", "parallel", "parallel")),
        name="rwkv_prep",
    )(pc, pc, pc, pc, p["mu_rkv"], p["mu_lora"], p["w0"], p["w2"], p["a0"], p["a2"],
      p["kk"], p["ka"], p["rk"])

    scan_spec = lambda: pl.BlockSpec((1, 1, TM, W_GRP),
                                     lambda bb, dr, i: (bb, dr, _scan_block(dr, i, nblk), 0))
    y = pl.pallas_call(
        functools.partial(_rwkv_scan_kernel, nblk),
        grid=(b, 2, nblk),
        in_specs=[scan_spec()] * 4,
        out_specs=scan_spec(),
        out_shape=sds,
        scratch_shapes=[pltpu.VMEM((RW_L, W_GRP), F32)],
        compiler_params=_params(("parallel", "parallel", "arbitrary")),
        name="rwkv_scan",
    )(gm, qm, rp, yp)

    return pl.pallas_call(
        _rwkv_merge_kernel,
        grid=(b, nblk),
        in_specs=[pl.BlockSpec((1, 2, TM, W_GRP), lambda bb, i: (bb, 0, i, 0)),
                  pl.BlockSpec((1, 2, TM, W_GRP), lambda bb, i: (bb, 0, i, 0)),
                  pl.BlockSpec((1, TM, W_GRP), lambda bb, i: (bb, i, 3)),
                  pl.BlockSpec((1, W_GRP), lambda bb, i: (0, 0)),
                  pl.BlockSpec((1, W_GRP), lambda bb, i: (0, 0))],
        out_specs=pl.BlockSpec((1, TM, W_GRP), lambda bb, i: (bb, i, 0)),
        out_shape=jax.ShapeDtypeStruct((b, t, W_GRP), BF16),
        compiler_params=_params(("parallel", "parallel")),
        name="rwkv_merge",
    )(y, bonus, pc, p["ln_w"], p["ln_b"])


def _outproj_kernel(mod_row, oa_ref, ob_ref, oc_ref, od_ref, w_ref, x_ref, mod_ref, g_ref, o_ref):
    d = x_ref.shape[-1]
    b = pl.program_id(0)
    acc = _mm(oa_ref[0], w_ref[0:W_GRP, :])
    acc = acc + _mm(ob_ref[0], w_ref[W_GRP:2 * W_GRP, :])
    acc = acc + _mm(oc_ref[0], w_ref[2 * W_GRP:3 * W_GRP, :])
    acc = acc + _mm(od_ref[0], w_ref[3 * W_GRP:, :])
    y = acc * lax.rsqrt(jnp.mean(acc * acc, axis=-1, keepdims=True) + EPS) * g_ref[...]
    row = mod_ref[0, pl.ds(b if mod_row is None else mod_row, 1), :]
    o_ref[0] = x_ref[0] + row[:, 2 * d:] * y


def _outproj(mix, w_out, x, mods, layer, g_post, blk0, mod_row):
    b, n, d = x.shape
    o_spec = pl.BlockSpec((1, TM, W_GRP), lambda bb, i: (bb, i + blk0, 0))
    return pl.pallas_call(
        functools.partial(_outproj_kernel, mod_row),
        grid=(b, n // TM),
        in_specs=[o_spec] * 4 + [
            pl.BlockSpec(w_out.shape, lambda bb, i: (0, 0)),
            pl.BlockSpec((1, TM, d), lambda bb, i: (bb, i, 0)),
            pl.BlockSpec((1, 8, 3 * d), lambda bb, i: (layer, 0, 0)),
            pl.BlockSpec((1, d), lambda bb, i: (0, 0))],
        out_specs=pl.BlockSpec((1, TM, d), lambda bb, i: (bb, i, 0)),
        out_shape=jax.ShapeDtypeStruct((b, n, d), F32),
        compiler_params=_params(("parallel", "parallel")),
        name="outproj",
    )(*mix, w_out, x, mods, g_post.reshape(1, d))


def _regroup_w_in(w):
    g = W_GRP
    o = [0]
    for s in (g, g, g, g, g, g, g, 3 * g, LORA_W + LORA_A, LORA_W + LORA_A, g, g, g, g, g):
        o.append(o[-1] + s)
    col = lambda j: w[:, o[j]:o[j + 1]]
    pad = jnp.zeros((w.shape[0], LORA_PAD - LORA_W - LORA_A), w.dtype)
    parts = [col(0), col(1), col(2), col(3), col(4), col(5), col(6),
             col(12), col(13), col(11), col(14),
             col(7), col(10), col(8), pad, col(9), pad]
    return jnp.concatenate(parts, axis=1).astype(BF16)


def _rope_tables(n_ctx, n_lat):
    pos = jnp.arange(n_lat)
    inv = ROPE_BASE ** (-jnp.arange(0, 32, 2, dtype=F32) / 32)
    lane = jnp.arange(HD_B)
    use_col = (lane // 32) % 2 == 1
    p = jnp.where(use_col[None, :], (pos % GRID_W)[:, None], (pos // GRID_W)[:, None]).astype(F32)
    ang = p * inv[lane % 16][None, :]
    first = (lane % 32 < 16)[None, :]
    cos, sin = jnp.cos(ang), jnp.sin(ang)
    sa = jnp.where(first, -sin, 0.0)
    sb = jnp.where(first, 0.0, sin)
    ident = lambda v: jnp.full((n_ctx, HD_B), v, F32)
    return (jnp.concatenate([ident(1.0), cos]), jnp.concatenate([ident(0.0), sa]),
            jnp.concatenate([ident(0.0), sb]))


def _rwkv_params(l, mu, w0, w2, a0, a2, kk, ka, rk, ln_w, ln_b):
    g = W_GRP
    zpad = lambda n: jnp.zeros((2, n), F32)
    mu_l = jnp.concatenate([mu[l][:, 3 * g:], zpad(LORA_PAD - LORA_W - LORA_A)], axis=1)
    w2p = jnp.concatenate([w2[l], jnp.zeros((2, LORA_PAD - LORA_W, g), F32)], axis=1)
    a2p = jnp.concatenate([jnp.zeros((2, LORA_W, g), F32), a2[l],
                           jnp.zeros((2, LORA_PAD - LORA_W - LORA_A, g), F32)], axis=1)
    r3 = lambda v: v.reshape(2, 1, -1)
    return dict(mu_rkv=r3(mu[l][:, :3 * g]), mu_lora=r3(mu_l), w0=r3(w0[l]), w2=w2p, a0=r3(a0[l]),
                a2=a2p, kk=r3(kk[l]), ka=r3(ka[l]), rk=r3(rk[l].reshape(2, g)),
                ln_w=ln_w[l].reshape(1, g), ln_b=ln_b[l].reshape(1, g))


def kernel(x, c, ctx, c_ctx, w_mod, b_mod, g_pre, g_post, w_in, w_out, sgu_w, sgu_b, lam_q1, lam_k1, lam_q2, lam_k2, subln_g, rwkv_mu, rwkv_w0, rwkv_w2, rwkv_a0, rwkv_a2, rwkv_kk, rwkv_ka, rwkv_rk, rwkv_ln_w, rwkv_ln_b, conv_w):
    depth = w_mod.shape[0]
    b, n_lat, _ = x.shape
    n_ctx = ctx.shape[1]
    mods = _modulation(c, c_ctx, w_mod, b_mod)
    rope = _rope_tables(n_ctx, n_lat)
    xc = ctx
    for l in range(depth):
        need_ctx = l < depth - 1
        lam_init = 0.8 - 0.6 * math.exp(-0.3 * l)
        pa, pb, pd, pc = _inproj(xc, x, mods, l, g_pre[l], _regroup_w_in(w_in[l]), rope)
        o_a, o_d = _local_mixers(pa, pd, sgu_w[l], sgu_b[l], conv_w[l])
        lam_params = jnp.stack([lam_q1[l], lam_k1[l], lam_q2[l], lam_k2[l]])
        o_b = _attention(pb, lam_params, subln_g[l], lam_init, 0 if need_ctx else 1)
        o_c = _rwkv(pc, _rwkv_params(l, rwkv_mu, rwkv_w0, rwkv_w2, rwkv_a0, rwkv_a2, rwkv_kk,
                                     rwkv_ka, rwkv_rk, rwkv_ln_w, rwkv_ln_b))
        mix = (o_a, o_b, o_c, o_d)
        w_o = w_out[l].astype(BF16)
        x_new = _outproj(mix, w_o, x, mods, l, g_post[l], n_ctx // TM, None)
        if need_ctx:
            xc = _outproj(mix, w_o, xc, mods, l, g_post[l], 0, b)
        x = x_new
    return x
```

```python
import functools
import math

import jax
import jax.numpy as jnp
from jax import lax
from jax.experimental import pallas as pl
from jax.experimental.pallas import tpu as pltpu

F32 = jnp.float32
BF16 = jnp.bfloat16

EPS = 1e-6
GN_EPS = 64e-5
GRID_W = 64
ROPE_BASE = 10000.0
LOG2E = 1.4426950408889634

W_GRP = 512
CHUNK_A = 128
HD_B = 128
HD_QK = 64
HD_C = 64
LORA_W = 64
LORA_A = 32
LORA_PAD = 128
RW_L = 64
RW_G = 256
TM = 256
VMEM_LIMIT = 56 * 1024 * 1024

N_A = 3 * W_GRP
N_B = 4 * W_GRP
N_D = 4 * W_GRP
N_C = 4 * W_GRP + 2 * LORA_PAD
OFF_A, OFF_B, OFF_D, OFF_C = 0, N_A, N_A + N_B, N_A + N_B + N_D
N_ALL = OFF_C + N_C


def _silu(g):
    return g * (1.0 / (1.0 + jnp.exp(-g)))


def _sigmoid(x):
    return 1.0 / (1.0 + jnp.exp(-x))


def _split_bf16(x):
    hi = x.astype(BF16)
    lo = (x - hi.astype(F32)).astype(BF16)
    return hi, lo


def _mm(a, b):
    return jnp.dot(a, b, preferred_element_type=F32)


def _mm_nt(a, b):
    return lax.dot_general(a, b, (((1,), (1,)), ((), ())), preferred_element_type=F32)


def _params(sem):
    return pltpu.CompilerParams(dimension_semantics=sem, vmem_limit_bytes=VMEM_LIMIT)


def _mod_kernel(s_ref, w_ref, b_ref, o_ref):
    s = _silu(s_ref[...])
    o_ref[0] = _mm(s.astype(BF16), w_ref[0].astype(BF16)) + b_ref[0]


def _modulation(c, c_ctx, w_mod, b_mod):
    depth, d, n3 = w_mod.shape
    b = c.shape[0]
    rows = jnp.zeros((8, d), F32).at[:b].set(c).at[b].set(c_ctx)
    tn = 768
    return pl.pallas_call(
        _mod_kernel,
        grid=(depth, n3 // tn),
        in_specs=[pl.BlockSpec((8, d), lambda l, j: (0, 0)),
                  pl.BlockSpec((1, d, tn), lambda l, j: (l, 0, j)),
                  pl.BlockSpec((1, 1, tn), lambda l, j: (l, 0, j))],
        out_specs=pl.BlockSpec((1, 8, tn), lambda l, j: (l, 0, j)),
        out_shape=jax.ShapeDtypeStruct((depth, 8, n3), F32),
        compiler_params=_params(("parallel", "parallel")),
        name="modulation",
    )(rows, w_mod, b_mod.reshape(depth, 1, n3))


def _inproj_kernel(n_batch, xc_ref, xl_ref, mod_ref, g_ref, w_ref, cos_ref, sa_ref, sb_ref,
                   pa_ref, pb_ref, pd_ref, pc_ref):
    b = pl.program_id(0)
    i = pl.program_id(1)
    d = xc_ref.shape[-1]
    is_ctx = i == 0
    x = jnp.where(is_ctx, xc_ref[0], xl_ref[0])
    y = x * lax.rsqrt(jnp.mean(x * x, axis=-1, keepdims=True) + EPS) * g_ref[...]
    row = jnp.where(is_ctx, mod_ref[0, pl.ds(n_batch, 1), :], mod_ref[0, pl.ds(b, 1), :])
    h = (y * (1.0 + row[:, d:2 * d]) + row[:, :d]).astype(BF16)

    def proj(c0, c1):
        return _mm(h, w_ref[:, c0:c1])

    pa_ref[0] = proj(OFF_A, OFF_A + N_A).astype(BF16)
    cos, sa, sb = cos_ref[...], sa_ref[...], sb_ref[...]
    qk = proj(OFF_B, OFF_B + 2 * W_GRP)
    for j in range(2 * W_GRP // HD_B):
        t = qk[:, j * HD_B:(j + 1) * HD_B]
        t = t * cos + pltpu.roll(t, HD_B - 16, axis=1) * sa + pltpu.roll(t, 16, axis=1) * sb
        if j < W_GRP // HD_B:
            t = t * (HD_QK ** -0.5 * LOG2E)
        pb_ref[0, :, j * HD_B:(j + 1) * HD_B] = t.astype(BF16)
    pb_ref[0, :, 2 * W_GRP:] = proj(OFF_B + 2 * W_GRP, OFF_B + N_B).astype(BF16)
    pd_ref[0] = proj(OFF_D, OFF_D + N_D).astype(BF16)
    pc_ref[0] = proj(OFF_C, OFF_C + N_C).astype(BF16)


def _inproj(xc, xl, mods, layer, g_pre, w_in, rope):
    b, n_ctx, d = xc.shape
    n_lat = xl.shape[1]
    t = n_ctx + n_lat
    assert n_ctx == TM and n_lat % TM == 0
    nblk = t // TM
    cos, sa, sb = rope
    tab = pl.BlockSpec((TM, HD_B), lambda bb, i: (i, 0))
    out = lambda n: pl.BlockSpec((1, TM, n), lambda bb, i: (bb, i, 0))
    return pl.pallas_call(
        functools.partial(_inproj_kernel, b),
        grid=(b, nblk),
        in_specs=[pl.BlockSpec((1, TM, d), lambda bb, i: (bb, 0, 0)),
                  pl.BlockSpec((1, TM, d), lambda bb, i: (bb, jnp.maximum(i - 1, 0), 0)),
                  pl.BlockSpec((1, 8, 3 * d), lambda bb, i: (layer, 0, 0)),
                  pl.BlockSpec((1, d), lambda bb, i: (0, 0)),
                  pl.BlockSpec((d, N_ALL), lambda bb, i: (0, 0), pipeline_mode=pl.Buffered(1)),
                  tab, tab, tab],
        out_specs=[out(N_A), out(N_B), out(N_D), out(N_C)],
        out_shape=[jax.ShapeDtypeStruct((b, t, n), BF16) for n in (N_A, N_B, N_D, N_C)],
        compiler_params=_params(("parallel", "arbitrary")),
        name="inproj",
    )(xc, xl, mods, g_pre.reshape(1, d), w_in, cos, sa, sb)


def _local_kernel(n_blk, pa_ref, pd_ref, hp_ref, hn_ref, ws_ref, bs_ref, cw_ref, oa_ref, od_ref):
    i = pl.program_id(1)
    for c in range(TM // CHUNK_A):
        rows = slice(c * CHUNK_A, (c + 1) * CHUNK_A)
        for h in range(W_GRP // CHUNK_A):
            cols = slice(h * CHUNK_A, (h + 1) * CHUNK_A)
            u = pa_ref[0, rows, cols].astype(F32)
            v = pa_ref[0, rows, W_GRP + h * CHUNK_A:W_GRP + (h + 1) * CHUNK_A].astype(F32)
            g = pa_ref[0, rows, 2 * W_GRP + h * CHUNK_A:2 * W_GRP + (h + 1) * CHUNK_A].astype(F32)
            mean = jnp.mean(v, axis=-1, keepdims=True)
            vc = v - mean
            vn = vc * lax.rsqrt(jnp.mean(vc * vc, axis=-1, keepdims=True) + EPS)
            mixed = _mm(ws_ref[h].astype(BF16), vn.astype(BF16)) + bs_ref[:, h:h + 1]
            oa_ref[0, rows, cols] = (u * mixed * _silu(g)).astype(BF16)
    cx = pd_ref[0, :, 0:2 * W_GRP].astype(F32)
    z = cx[:, :W_GRP] * cx[:, W_GRP:]
    hp = hp_ref[0].astype(F32)
    hn = hn_ref[0].astype(F32)
    zp = hp[15:16, :W_GRP] * hp[15:16, W_GRP:]
    zn = hn[0:1, :W_GRP] * hn[0:1, W_GRP:]
    zp = jnp.where(i <= 1, 0.0, zp)
    zn = jnp.where(jnp.logical_or(i == 0, i == n_blk - 1), 0.0, zn)
    r = lax.broadcasted_iota(jnp.int32, (TM, 1), 0)
    z_prev = jnp.where(r == 0, zp, pltpu.roll(z, 1, axis=0))
    z_next = jnp.where(r == TM - 1, zn, pltpu.roll(z, TM - 1, axis=0))
    w = cw_ref[...]
    y = z_prev * w[0:1] + z * w[1:2] + z_next * w[2:3]
    bg = pd_ref[0, :, 2 * W_GRP:3 * W_GRP].astype(F32)
    g = pd_ref[0, :, 3 * W_GRP:].astype(F32)
    od_ref[0] = (bg * y * _silu(g)).astype(BF16)


def _local_mixers(pa, pd, sgu_w, sgu_b, conv_w):
    b, t, _ = pa.shape
    nblk = t // TM
    hb = TM // 16
    return pl.pallas_call(
        functools.partial(_local_kernel, nblk),
        grid=(b, nblk),
        in_specs=[pl.BlockSpec((1, TM, N_A), lambda bb, i: (bb, i, 0)),
                  pl.BlockSpec((1, TM, N_D), lambda bb, i: (bb, i, 0)),
                  pl.BlockSpec((1, 16, 2 * W_GRP), lambda bb, i: (bb, jnp.maximum(i * hb - 1, 0), 0)),
                  pl.BlockSpec((1, 16, 2 * W_GRP),
                               lambda bb, i: (bb, jnp.minimum((i + 1) * hb, nblk * hb - 1), 0)),
                  pl.BlockSpec(sgu_w.shape, lambda bb, i: (0, 0, 0)),
                  pl.BlockSpec((CHUNK_A, sgu_w.shape[0]), lambda bb, i: (0, 0)),
                  pl.BlockSpec(conv_w.shape, lambda bb, i: (0, 0))],
        out_specs=[pl.BlockSpec((1, TM, W_GRP), lambda bb, i: (bb, i, 0))] * 2,
        out_shape=[jax.ShapeDtypeStruct((b, t, W_GRP), BF16)] * 2,
        compiler_params=_params(("parallel", "parallel")),
        name="local_mixers",
    )(pa, pd, pd, pd, sgu_w, sgu_b.T, conv_w)


def _attn_kernel(lam_init, ctx_first, q_ref, k_ref, v_ref, g_ref, lam_ref, sg_ref, o_ref, vext_ref):
    tq = q_ref.shape[1]
    qi = pl.program_id(2)

    @pl.when(qi == 0)
    def _():
        vext_ref[:, :HD_B] = v_ref[0]
        vext_ref[:, HD_B:] = jnp.ones((vext_ref.shape[0], HD_B), BF16)

    lp = lam_ref[...]
    lam = (jnp.exp(jnp.sum(lp[0:1] * lp[1:2], axis=-1, keepdims=True))
           - jnp.exp(jnp.sum(lp[2:3] * lp[3:4], axis=-1, keepdims=True)) + lam_init)
    q = q_ref[0]
    lane = lax.broadcasted_iota(jnp.int32, (1, HD_B), 1)
    zero = jnp.zeros_like(q)
    q2 = jnp.concatenate([jnp.where(lane < HD_QK, q, zero), jnp.where(lane >= HD_QK, q, zero)], axis=0)

    def attend(n_keys):
        s = _mm_nt(q2, k_ref[0, :n_keys])
        p = jnp.exp2(s - jnp.max(s, axis=-1, keepdims=True)).astype(BF16)
        o = _mm(p, vext_ref[:n_keys])
        a = o[:tq, :HD_B] / o[:tq, HD_B:] - lam * (o[tq:, :HD_B] / o[tq:, HD_B:])
        y = a * lax.rsqrt(jnp.mean(a * a, axis=-1, keepdims=True) + EPS) * sg_ref[...] * (1.0 - lam_init)
        o_ref[0] = (y * _silu(g_ref[0].astype(F32))).astype(BF16)

    if ctx_first:
        pl.when(qi == 0)(lambda: attend(tq))
        pl.when(qi != 0)(lambda: attend(k_ref.shape[1]))
    else:
        attend(k_ref.shape[1])


def _attention(pb, lam_params, subln_g, lam_init, q_blk0):
    b, t, _ = pb.shape
    nh = W_GRP // HD_B
    nq = t // TM - q_blk0
    return pl.pallas_call(
        functools.partial(_attn_kernel, lam_init, q_blk0 == 0),
        grid=(b, nh, nq),
        in_specs=[pl.BlockSpec((1, TM, HD_B), lambda bb, h, i: (bb, i + q_blk0, h)),
                  pl.BlockSpec((1, t, HD_B), lambda bb, h, i: (bb, 0, nh + h)),
                  pl.BlockSpec((1, t, HD_B), lambda bb, h, i: (bb, 0, 2 * nh + h)),
                  pl.BlockSpec((1, TM, HD_B), lambda bb, h, i: (bb, i + q_blk0, 3 * nh + h)),
                  pl.BlockSpec((4, HD_QK), lambda bb, h, i: (0, 0)),
                  pl.BlockSpec((1, HD_B), lambda bb, h, i: (0, 0))],
        out_specs=pl.BlockSpec((1, TM, HD_B), lambda bb, h, i: (bb, i + q_blk0, h)),
        out_shape=jax.ShapeDtypeStruct((b, t, W_GRP), BF16),
        scratch_shapes=[pltpu.VMEM((t, 2 * HD_B), BF16)],
        compiler_params=_params(("parallel", "parallel", "arbitrary")),
        name="diff_attention",
    )(pb, pb, pb, pb, lam_params, subln_g.reshape(1, HD_B))


def _seg_sum(x, ones_bd):
    hi, lo = _split_bf16(x)
    return _mm(hi, ones_bd) + _mm(lo, ones_bd)


def _head_diag(x4, mask):
    return jnp.where(mask, x4, 0.0).astype(BF16)


def _rwkv_prep_kernel(n_blk, rkv_ref, lo_ref, hrkv_ref, hlo_ref, mu_ref, mul_ref, w0_ref, w2_ref,
                      a0_ref, a2_ref, kk_ref, ka_ref, rk_ref,
                      g_ref, q_ref, rp_ref, yp_ref, bonus_ref):
    dr = pl.program_id(1)
    i = pl.program_id(2)
    fwd = dr == 0
    nl = RW_G // HD_C
    sgn = 1 - 2 * dr
    hrow = jnp.where(fwd, 15, 0)
    edge = jnp.logical_or(i == 0, i == jnp.where(fwd, 1, n_blk - 1))
    rsel = lax.broadcasted_iota(jnp.int32, (TM, 1), 0) == jnp.where(fwd, 0, TM - 1)
    hsel = lax.broadcasted_iota(jnp.int32, (16, 1), 0) == hrow

    def shifted(cur_ref, halo_ref, mu):
        f = cur_ref[0].astype(F32)
        halo = jnp.sum(jnp.where(hsel, halo_ref[0].astype(F32), 0.0), axis=0, keepdims=True)
        halo = jnp.where(edge, 0.0, halo)
        rolled = jnp.where(fwd, pltpu.roll(f, 1, axis=0), pltpu.roll(f, TM - 1, axis=0))
        prev = jnp.where(rsel, halo, rolled)
        return f + (prev - f) * mu

    z = shifted(rkv_ref, hrkv_ref, mu_ref[0])
    zl = shifted(lo_ref, hlo_ref, mul_ref[0])
    r, k, v = z[:, :W_GRP], z[:, W_GRP:2 * W_GRP], z[:, 2 * W_GRP:]
    wl = w0_ref[0] + _mm(jnp.tanh(zl).astype(BF16), w2_ref[0].astype(BF16))
    logd = -math.exp(-0.5) * _sigmoid(wl)
    a = _sigmoid(a0_ref[0] + _mm(zl.astype(BF16), a2_ref[0].astype(BF16)))
    ri = lax.broadcasted_iota(jnp.int32, (W_GRP, W_GRP), 0) // HD_C
    ci = lax.broadcasted_iota(jnp.int32, (W_GRP, W_GRP), 1) // HD_C
    ones_bd = jnp.where(ri == ci, 1.0, 0.0).astype(BF16)
    kk = k * kk_ref[0]
    kk = kk / jnp.maximum(jnp.sqrt(_seg_sum(kk * kk, ones_bd)), 1e-12)
    k2 = k * (1.0 + (a - 1.0) * ka_ref[0])
    bv = kk * a
    bonus_ref[0, 0] = _seg_sum(r * k2 * rk_ref[0], ones_bd) * v
    rt = lax.broadcasted_iota(jnp.int32, (TM, TM), 0)
    ct = lax.broadcasted_iota(jnp.int32, (TM, TM), 1)
    same = rt // RW_L == ct // RW_L
    tri = jnp.where(jnp.logical_and(same, (rt - ct) * sgn >= 0), 1.0, 0.0).astype(BF16)
    blk = jnp.where(same, 1.0, 0.0).astype(BF16)
    lh, ll = _split_bf16(logd)
    cum = _mm(tri, lh) + _mm(tri, ll)
    tot = _mm(blk, lh) + _mm(blk, ll)
    half = 0.5 * tot
    e_pos = jnp.exp(cum - half)
    e_neg = jnp.exp(half - cum)
    e_prev = jnp.exp(cum - logd - half)
    e_end = jnp.exp(tot - cum)
    g_end = jnp.exp(tot)
    e_half = jnp.exp(half)
    kt_all = kk * e_prev
    rt_all = r * e_pos
    kh_all = k2 * e_neg
    bh_all = bv * e_neg
    kb_all = k2 * e_end
    bb_all = bv * e_end
    row = lax.broadcasted_iota(jnp.int32, (RW_L, RW_G), 0)
    col = lax.broadcasted_iota(jnp.int32, (RW_L, RW_G), 1) % HD_C
    eye = jnp.where(col == row, 1.0, 0.0)
    r4 = lax.broadcasted_iota(jnp.int32, (RW_G, RW_G), 0) // HD_C
    c4 = lax.broadcasted_iota(jnp.int32, (RW_G, RW_G), 1) // HD_C
    dmask = r4 == c4
    r2 = lax.broadcasted_iota(jnp.int32, (2 * RW_L, RW_G), 0) % RW_L
    c2 = lax.broadcasted_iota(jnp.int32, (2 * RW_L, RW_G), 1) % HD_C
    first = jnp.where(lax.broadcasted_iota(jnp.int32, (2 * RW_L, RW_G), 0) < RW_L, 1, 0)
    mask2 = (r2 - c2) * sgn >= first

    def bd(x):
        return _head_diag(jnp.concatenate([x] * nl, axis=0), dmask)

    def hprod(x, y):
        return _mm(x.astype(BF16), bd(y))

    def diag_blocks(full):
        out = jnp.zeros((RW_L, RW_G), F32)
        lane_h = lax.broadcasted_iota(jnp.int32, (RW_L, RW_G), 1) // HD_C
        for hh in range(nl):
            out = out + jnp.where(lane_h == hh, full[hh * HD_C:(hh + 1) * HD_C, :], 0.0)
        return out

    pairs = [(slice(c * RW_L, (c + 1) * RW_L), slice(gi * RW_G, (gi + 1) * RW_G))
             for c in range(TM // RW_L) for gi in range(W_GRP // RW_G)]
    cat = lambda *xs: jnp.concatenate(xs, axis=0)
    L = RW_L
    kt = [kt_all[p] for p in pairs]
    rtl = [rt_all[p] for p in pairs]
    vv = [v[p] for p in pairs]
    lhs = [cat(a_, b_).astype(BF16) for a_, b_ in zip(kt, rtl)]
    pb_ = [jnp.where(mask2, _mm_nt(x, bd(bh_all[p])), 0.0) for x, p in zip(lhs, pairs)]
    pk_ = [jnp.where(mask2, _mm_nt(x, bd(kh_all[p])), 0.0) for x, p in zip(lhs, pairs)]
    res = [hprod(x, x[:L]) for x in pb_]
    tinv = [eye - x[:L] for x in pb_]
    mt = [x[L:] - y[L:] for x, y in zip(pb_, res)]
    pw = [y[:L] for y in res]
    for _ in range(4):
        res = [hprod(cat(t_, m_, w_), w_) for t_, m_, w_ in zip(tinv, mt, pw)]
        tinv = [t_ + y[:L] for t_, y in zip(tinv, res)]
        mt = [m_ + y[L:2 * L] for m_, y in zip(mt, res)]
        pw = [y[2 * L:] for y in res]
    res = [hprod(cat(t_, m_), w_) for t_, m_, w_ in zip(tinv, mt, pw)]
    tm = [cat(t_, m_) + y for t_, m_, y in zip(tinv, mt, res)]
    wv = [hprod(x, y) for x, y in zip(pk_, vv)]
    tmb = [x.astype(BF16) for x in tm]
    ww = [_mm(x, bd(y)) for x, y in zip(tmb, kt)]
    uu = [_mm(x, bd(y[:L])) for x, y in zip(tmb, wv)]
    for j, p in enumerate(pairs):
        rows, cols = p
        eh = e_half[rows.start:rows.start + 1, cols]
        rp_ref[0, 0, rows, cols] = (rtl[j] - ww[j][L:]) * eh
        yp_ref[0, 0, rows, cols] = wv[j][L:] - uu[j][L:]
        xt = cat(kb_all[p], -bb_all[p]).T.astype(BF16)
        zeros = jnp.zeros((L, RW_G), F32)
        yq = cat(vv[j], uu[j][:L]).astype(BF16)
        yg = cat(zeros, ww[j][:L]).astype(BF16)
        q_ref[0, 0, rows, cols] = diag_blocks(_mm(xt, yq))
        g_ref[0, 0, rows, cols] = (diag_blocks(_mm(xt, yg)) * eh
                                   + eye * g_end[rows.start:rows.start + 1, cols])


def _rwkv_scan_kernel(n_blk, g_ref, q_ref, rp_ref, yp_ref, y_ref, h_ref):
    nl = RW_G // HD_C

    @pl.when(pl.program_id(2) == 0)
    def _():
        h_ref[...] = jnp.zeros_like(h_ref)

    dr = pl.program_id(1)
    r4 = lax.broadcasted_iota(jnp.int32, (RW_G, RW_G), 0) // HD_C
    c4 = lax.broadcasted_iota(jnp.int32, (RW_G, RW_G), 1) // HD_C
    dmask = r4 == c4
    n_chunk = TM // RW_L
    for s in range(n_chunk):
        c = jnp.where(dr == 0, s, n_chunk - 1 - s)
        rows = pl.ds(pl.multiple_of(c * RW_L, RW_L), RW_L)
        for gi in range(W_GRP // RW_G):
            cols = slice(gi * RW_G, (gi + 1) * RW_G)
            h = h_ref[:, cols]
            h4 = jnp.where(dmask, jnp.concatenate([h] * nl, axis=0), 0.0)
            hh, hl = _split_bf16(h4)
            lhs = jnp.concatenate([g_ref[0, 0, rows, cols], rp_ref[0, 0, rows, cols]], axis=0)
            lh, ll = _split_bf16(lhs)
            prod = _mm(lh, hh) + _mm(lh, hl) + _mm(ll, hh)
            y_ref[0, 0, rows, cols] = prod[RW_L:] + yp_ref[0, 0, rows, cols]
            h_ref[:, cols] = prod[:RW_L] + q_ref[0, 0, rows, cols]


def _rwkv_merge_kernel(y_ref, bonus_ref, g_ref, lw_ref, lb_ref, o_ref):
    ri = lax.broadcasted_iota(jnp.int32, (W_GRP, W_GRP), 0) // HD_C
    ci = lax.broadcasted_iota(jnp.int32, (W_GRP, W_GRP), 1) // HD_C
    ones_bd = jnp.where(ri == ci, 1.0, 0.0).astype(BF16)
    y = y_ref[0, 0] + y_ref[0, 1]
    mean = _seg_sum(y, ones_bd) * (1.0 / HD_C)
    yc = y - mean
    var = _seg_sum(yc * yc, ones_bd) * (1.0 / HD_C)
    yn = yc * lax.rsqrt(var + GN_EPS) * lw_ref[...] + lb_ref[...]
    out = yn + bonus_ref[0, 0] + bonus_ref[0, 1]
    o_ref[0] = (out * _silu(g_ref[0].astype(F32))).astype(BF16)


def _scan_block(dr, i, n_blk):
    return jnp.where(dr == 0, i, jnp.where(i == 0, 0, n_blk - i))


def _rwkv(pc, p):
    b, t, _ = pc.shape
    nblk = t // TM
    hb = TM // 16
    lblk = 4 * W_GRP // LORA_PAD

    def halo_idx(dr, i):
        return jnp.where(dr == 0, jnp.maximum(i * hb - 1, 0), jnp.minimum((i + 1) * hb, nblk * hb - 1))

    per_dir = lambda n: pl.BlockSpec((1, 1, n), lambda bb, dr, i: (dr, 0, 0))
    per_dir_m = lambda m, n: pl.BlockSpec((1, m, n), lambda bb, dr, i: (dr, 0, 0))
    state = lambda: pl.BlockSpec((1, 1, TM, W_GRP), lambda bb, dr, i: (bb, dr, i, 0))
    sds = jax.ShapeDtypeStruct((b, 2, t, W_GRP), F32)
    gm, qm, rp, yp, bonus = pl.pallas_call(
        functools.partial(_rwkv_prep_kernel, nblk),
        grid=(b, 2, nblk),
        in_specs=[pl.BlockSpec((1, TM, 3 * W_GRP), lambda bb, dr, i: (bb, i, 0)),
                  pl.BlockSpec((1, TM, LORA_PAD), lambda bb, dr, i: (bb, i, lblk + dr)),
                  pl.BlockSpec((1, 16, 3 * W_GRP), lambda bb, dr, i: (bb, halo_idx(dr, i), 0)),
                  pl.BlockSpec((1, 16, LORA_PAD), lambda bb, dr, i: (bb, halo_idx(dr, i), lblk + dr)),
                  per_dir(3 * W_GRP), per_dir(LORA_PAD), per_dir(W_GRP), per_dir_m(LORA_PAD, W_GRP),
                  per_dir(W_GRP), per_dir_m(LORA_PAD, W_GRP), per_dir(W_GRP), per_dir(W_GRP),
                  per_dir(W_GRP)],
        out_specs=[state()] * 5,
        out_shape=[sds] * 5,
        compiler_params=_params(("parallel", "parallel", "parallel")),
        name="rwkv_prep",
    )(pc, pc, pc, pc, p["mu_rkv"], p["mu_lora"], p["w0"], p["w2"], p["a0"], p["a2"],
      p["kk"], p["ka"], p["rk"])

    scan_spec = lambda: pl.BlockSpec((1, 1, TM, W_GRP),
                                     lambda bb, dr, i: (bb, dr, _scan_block(dr, i, nblk), 0))
    y = pl.pallas_call(
        functools.partial(_rwkv_scan_kernel, nblk),
        grid=(b, 2, nblk),
        in_specs=[scan_spec()] * 4,
        out_specs=scan_spec(),
        out_shape=sds,
        scratch_shapes=[pltpu.VMEM((RW_L, W_GRP), F32)],
        compiler_params=_params(("parallel", "parallel", "arbitrary")),
        name="rwkv_scan",
    )(gm, qm, rp, yp)

    return pl.pallas_call(
        _rwkv_merge_kernel,
        grid=(b, nblk),
        in_specs=[pl.BlockSpec((1, 2, TM, W_GRP), lambda bb, i: (bb, 0, i, 0)),
                  pl.BlockSpec((1, 2, TM, W_GRP), lambda bb, i: (bb, 0, i, 0)),
                  pl.BlockSpec((1, TM, W_GRP), lambda bb, i: (bb, i, 3)),
                  pl.BlockSpec((1, W_GRP), lambda bb, i: (0, 0)),
                  pl.BlockSpec((1, W_GRP), lambda bb, i: (0, 0))],
        out_specs=pl.BlockSpec((1, TM, W_GRP), lambda bb, i: (bb, i, 0)),
        out_shape=jax.ShapeDtypeStruct((b, t, W_GRP), BF16),
        compiler_params=_params(("parallel", "parallel")),
        name="rwkv_merge",
    )(y, bonus, pc, p["ln_w"], p["ln_b"])


def _outproj_kernel(mod_row, oa_ref, ob_ref, oc_ref, od_ref, w_ref, x_ref, mod_ref, g_ref, o_ref):
    d = x_ref.shape[-1]
    b = pl.program_id(0)
    acc = _mm(oa_ref[0], w_ref[0:W_GRP, :])
    acc = acc + _mm(ob_ref[0], w_ref[W_GRP:2 * W_GRP, :])
    acc = acc + _mm(oc_ref[0], w_ref[2 * W_GRP:3 * W_GRP, :])
    acc = acc + _mm(od_ref[0], w_ref[3 * W_GRP:, :])
    y = acc * lax.rsqrt(jnp.mean(acc * acc, axis=-1, keepdims=True) + EPS) * g_ref[...]
    row = mod_ref[0, pl.ds(b if mod_row is None else mod_row, 1), :]
    o_ref[0] = x_ref[0] + row[:, 2 * d:] * y


def _outproj(mix, w_out, x, mods, layer, g_post, blk0, mod_row):
    b, n, d = x.shape
    o_spec = pl.BlockSpec((1, TM, W_GRP), lambda bb, i: (bb, i + blk0, 0))
    return pl.pallas_call(
        functools.partial(_outproj_kernel, mod_row),
        grid=(b, n // TM),
        in_specs=[o_spec] * 4 + [
            pl.BlockSpec(w_out.shape, lambda bb, i: (0, 0)),
            pl.BlockSpec((1, TM, d), lambda bb, i: (bb, i, 0)),
            pl.BlockSpec((1, 8, 3 * d), lambda bb, i: (layer, 0, 0)),
            pl.BlockSpec((1, d), lambda bb, i: (0, 0))],
        out_specs=pl.BlockSpec((1, TM, d), lambda bb, i: (bb, i, 0)),
        out_shape=jax.ShapeDtypeStruct((b, n, d), F32),
        compiler_params=_params(("parallel", "parallel")),
        name="outproj",
    )(*mix, w_out, x, mods, g_post.reshape(1, d))


def _regroup_w_in(w):
    g = W_GRP
    o = [0]
    for s in (g, g, g, g, g, g, g, 3 * g, LORA_W + LORA_A, LORA_W + LORA_A, g, g, g, g, g):
        o.append(o[-1] + s)
    col = lambda j: w[:, o[j]:o[j + 1]]
    pad = jnp.zeros((w.shape[0], LORA_PAD - LORA_W - LORA_A), w.dtype)
    parts = [col(0), col(1), col(2), col(3), col(4), col(5), col(6),
             col(12), col(13), col(11), col(14),
             col(7), col(10), col(8), pad, col(9), pad]
    return jnp.concatenate(parts, axis=1).astype(BF16)


def _rope_tables(n_ctx, n_lat):
    pos = jnp.arange(n_lat)
    inv = ROPE_BASE ** (-jnp.arange(0, 32, 2, dtype=F32) / 32)
    lane = jnp.arange(HD_B)
    use_col = (lane // 32) % 2 == 1
    p = jnp.where(use_col[None, :], (pos % GRID_W)[:, None], (pos // GRID_W)[:, None]).astype(F32)
    ang = p * inv[lane % 16][None, :]
    first = (lane % 32 < 16)[None, :]
    cos, sin = jnp.cos(ang), jnp.sin(ang)
    sa = jnp.where(first, -sin, 0.0)
    sb = jnp.where(first, 0.0, sin)
    ident = lambda v: jnp.full((n_ctx, HD_B), v, F32)
    return (jnp.concatenate([ident(1.0), cos]), jnp.concatenate([ident(0.0), sa]),
            jnp.concatenate([ident(0.0), sb]))


def _rwkv_params(l, mu, w0, w2, a0, a2, kk, ka, rk, ln_w, ln_b):
    g = W_GRP
    zpad = lambda n: jnp.zeros((2, n), F32)
    mu_l = jnp.concatenate([mu[l][:, 3 * g:], zpad(LORA_PAD - LORA_W - LORA_A)], axis=1)
    w2p = jnp.concatenate([w2[l], jnp.zeros((2, LORA_PAD - LORA_W, g), F32)], axis=1)
    a2p = jnp.concatenate([jnp.zeros((2, LORA_W, g), F32), a2[l],
                           jnp.zeros((2, LORA_PAD - LORA_W - LORA_A, g), F32)], axis=1)
    r3 = lambda v: v.reshape(2, 1, -1)
    return dict(mu_rkv=r3(mu[l][:, :3 * g]), mu_lora=r3(mu_l), w0=r3(w0[l]), w2=w2p, a0=r3(a0[l]),
                a2=a2p, kk=r3(kk[l]), ka=r3(ka[l]), rk=r3(rk[l].reshape(2, g)),
                ln_w=ln_w[l].reshape(1, g), ln_b=ln_b[l].reshape(1, g))


def kernel(x, c, ctx, c_ctx, w_mod, b_mod, g_pre, g_post, w_in, w_out, sgu_w, sgu_b, lam_q1, lam_k1, lam_q2, lam_k2, subln_g, rwkv_mu, rwkv_w0, rwkv_w2, rwkv_a0, rwkv_a2, rwkv_kk, rwkv_ka, rwkv_rk, rwkv_ln_w, rwkv_ln_b, conv_w):
    depth = w_mod.shape[0]
    b, n_lat, _ = x.shape
    n_ctx = ctx.shape[1]
    mods = _modulation(c, c_ctx, w_mod, b_mod)
    rope = _rope_tables(n_ctx, n_lat)
    xc = ctx
    for l in range(depth):
        need_ctx = l < depth - 1
        lam_init = 0.8 - 0.6 * math.exp(-0.3 * l)
        pa, pb, pd, pc = _inproj(xc, x, mods, l, g_pre[l], _regroup_w_in(w_in[l]), rope)
        o_a, o_d = _local_mixers(pa, pd, sgu_w[l], sgu_b[l], conv_w[l])
        lam_params = jnp.stack([lam_q1[l], lam_k1[l], lam_q2[l], lam_k2[l]])
        o_b = _attention(pb, lam_params, subln_g[l], lam_init, 0 if need_ctx else 1)
        o_c = _rwkv(pc, _rwkv_params(l, rwkv_mu, rwkv_w0, rwkv_w2, rwkv_a0, rwkv_a2, rwkv_kk,
                                     rwkv_ka, rwkv_rk, rwkv_ln_w, rwkv_ln_b))
        mix = (o_a, o_b, o_c, o_d)
        w_o = w_out[l].astype(BF16)
        x_new = _outproj(mix, w_o, x, mods, l, g_post[l], n_ctx // TM, None)
        if need_ctx:
            xc = _outproj(mix, w_o, xc, mods, l, g_post[l], 0, b)
        x = x_new
    return x
```

```python
import functools
import math

import jax
import jax.numpy as jnp
from jax import lax
from jax.experimental import pallas as pl
from jax.experimental.pallas import tpu as pltpu

F32 = jnp.float32
BF16 = jnp.bfloat16

EPS = 1e-6
GN_EPS = 64e-5
GRID_W = 64
ROPE_BASE = 10000.0
LOG2E = 1.4426950408889634

W_GRP = 512
CHUNK_A = 128
HD_B = 128
HD_QK = 64
HD_C = 64
LORA_W = 64
LORA_A = 32
LORA_PAD = 128
RW_L = 64
RW_G = 256
TM = 256
ATT_KB = 1024
VMEM_LIMIT = 56 * 1024 * 1024

N_A = 3 * W_GRP
N_B = 4 * W_GRP
N_D = 4 * W_GRP
N_C = 4 * W_GRP + 2 * LORA_PAD
OFF_A, OFF_B, OFF_D, OFF_C = 0, N_A, N_A + N_B, N_A + N_B + N_D
N_ALL = OFF_C + N_C


def _silu(g):
    return g * (1.0 / (1.0 + jnp.exp(-g)))


def _sigmoid(x):
    return 1.0 / (1.0 + jnp.exp(-x))


def _split_bf16(x):
    hi = x.astype(BF16)
    lo = (x - hi.astype(F32)).astype(BF16)
    return hi, lo


def _mm(a, b):
    return jnp.dot(a, b, preferred_element_type=F32)


def _mm_nt(a, b):
    return lax.dot_general(a, b, (((1,), (1,)), ((), ())), preferred_element_type=F32)


def _params(sem):
    return pltpu.CompilerParams(dimension_semantics=sem, vmem_limit_bytes=VMEM_LIMIT)


def _mod_kernel(s_ref, w_ref, b_ref, o_ref):
    s = _silu(s_ref[...])
    o_ref[0] = _mm(s.astype(BF16), w_ref[0].astype(BF16)) + b_ref[0]


def _modulation(c, c_ctx, w_mod, b_mod):
    depth, d, n3 = w_mod.shape
    b = c.shape[0]
    rows = jnp.zeros((8, d), F32).at[:b].set(c).at[b].set(c_ctx)
    tn = 768
    return pl.pallas_call(
        _mod_kernel,
        grid=(depth, n3 // tn),
        in_specs=[pl.BlockSpec((8, d), lambda l, j: (0, 0)),
                  pl.BlockSpec((1, d, tn), lambda l, j: (l, 0, j)),
                  pl.BlockSpec((1, 1, tn), lambda l, j: (l, 0, j))],
        out_specs=pl.BlockSpec((1, 8, tn), lambda l, j: (l, 0, j)),
        out_shape=jax.ShapeDtypeStruct((depth, 8, n3), F32),
        compiler_params=_params(("parallel", "parallel")),
        name="modulation",
    )(rows, w_mod, b_mod.reshape(depth, 1, n3))


def _inproj_kernel(n_batch, xc_ref, xl_ref, mod_ref, g_ref, w_ref, cos_ref, sa_ref, sb_ref,
                   pa_ref, pb_ref, pd_ref, pc_ref):
    b = pl.program_id(0)
    i = pl.program_id(1)
    d = xc_ref.shape[-1]
    is_ctx = i == 0
    x = jnp.where(is_ctx, xc_ref[0], xl_ref[0])
    y = x * lax.rsqrt(jnp.mean(x * x, axis=-1, keepdims=True) + EPS) * g_ref[...]
    row = jnp.where(is_ctx, mod_ref[0, pl.ds(n_batch, 1), :], mod_ref[0, pl.ds(b, 1), :])
    h = (y * (1.0 + row[:, d:2 * d]) + row[:, :d]).astype(BF16)

    def proj(c0, c1):
        return _mm(h, w_ref[:, c0:c1])

    pa_ref[0] = proj(OFF_A, OFF_A + N_A).astype(BF16)
    cos, sa, sb = cos_ref[...], sa_ref[...], sb_ref[...]
    qk = proj(OFF_B, OFF_B + 2 * W_GRP)
    for j in range(2 * W_GRP // HD_B):
        t = qk[:, j * HD_B:(j + 1) * HD_B]
        t = t * cos + pltpu.roll(t, HD_B - 16, axis=1) * sa + pltpu.roll(t, 16, axis=1) * sb
        if j < W_GRP // HD_B:
            t = t * (HD_QK ** -0.5 * LOG2E)
        pb_ref[0, :, j * HD_B:(j + 1) * HD_B] = t.astype(BF16)
    pb_ref[0, :, 2 * W_GRP:] = proj(OFF_B + 2 * W_GRP, OFF_B + N_B).astype(BF16)
    pd_ref[0] = proj(OFF_D, OFF_D + N_D).astype(BF16)
    pc_ref[0] = proj(OFF_C, OFF_C + N_C).astype(BF16)


def _inproj(xc, xl, mods, layer, g_pre, w_in, rope):
    b, n_ctx, d = xc.shape
    n_lat = xl.shape[1]
    t = n_ctx + n_lat
    assert n_ctx == TM and n_lat % TM == 0
    nblk = t // TM
    cos, sa, sb = rope
    tab = pl.BlockSpec((TM, HD_B), lambda bb, i: (i, 0))
    out = lambda n: pl.BlockSpec((1, TM, n), lambda bb, i: (bb, i, 0))
    return pl.pallas_call(
        functools.partial(_inproj_kernel, b),
        grid=(b, nblk),
        in_specs=[pl.BlockSpec((1, TM, d), lambda bb, i: (bb, 0, 0)),
                  pl.BlockSpec((1, TM, d), lambda bb, i: (bb, jnp.maximum(i - 1, 0), 0)),
                  pl.BlockSpec((1, 8, 3 * d), lambda bb, i: (layer, 0, 0)),
                  pl.BlockSpec((1, d), lambda bb, i: (0, 0)),
                  pl.BlockSpec((d, N_ALL), lambda bb, i: (0, 0), pipeline_mode=pl.Buffered(1)),
                  tab, tab, tab],
        out_specs=[out(N_A), out(N_B), out(N_D), out(N_C)],
        out_shape=[jax.ShapeDtypeStruct((b, t, n), BF16) for n in (N_A, N_B, N_D, N_C)],
        compiler_params=_params(("parallel", "arbitrary")),
        name="inproj",
    )(xc, xl, mods, g_pre.reshape(1, d), w_in, cos, sa, sb)


def _local_kernel(n_blk, pa_ref, pd_ref, hp_ref, hn_ref, ws_ref, bs_ref, cw_ref, oa_ref, od_ref):
    i = pl.program_id(1)
    for c in range(TM // CHUNK_A):
        rows = slice(c * CHUNK_A, (c + 1) * CHUNK_A)
        for h in range(W_GRP // CHUNK_A):
            cols = slice(h * CHUNK_A, (h + 1) * CHUNK_A)
            u = pa_ref[0, rows, cols].astype(F32)
            v = pa_ref[0, rows, W_GRP + h * CHUNK_A:W_GRP + (h + 1) * CHUNK_A].astype(F32)
            g = pa_ref[0, rows, 2 * W_GRP + h * CHUNK_A:2 * W_GRP + (h + 1) * CHUNK_A].astype(F32)
            mean = jnp.mean(v, axis=-1, keepdims=True)
            vc = v - mean
            vn = vc * lax.rsqrt(jnp.mean(vc * vc, axis=-1, keepdims=True) + EPS)
            mixed = _mm(ws_ref[h].astype(BF16), vn.astype(BF16)) + bs_ref[:, h:h + 1]
            oa_ref[0, rows, cols] = (u * mixed * _silu(g)).astype(BF16)
    cx = pd_ref[0, :, 0:2 * W_GRP].astype(F32)
    z = cx[:, :W_GRP] * cx[:, W_GRP:]
    hp = hp_ref[0].astype(F32)
    hn = hn_ref[0].astype(F32)
    zp = hp[15:16, :W_GRP] * hp[15:16, W_GRP:]
    zn = hn[0:1, :W_GRP] * hn[0:1, W_GRP:]
    zp = jnp.where(i <= 1, 0.0, zp)
    zn = jnp.where(jnp.logical_or(i == 0, i == n_blk - 1), 0.0, zn)
    r = lax.broadcasted_iota(jnp.int32, (TM, 1), 0)
    z_prev = jnp.where(r == 0, zp, pltpu.roll(z, 1, axis=0))
    z_next = jnp.where(r == TM - 1, zn, pltpu.roll(z, TM - 1, axis=0))
    w = cw_ref[...]
    y = z_prev * w[0:1] + z * w[1:2] + z_next * w[2:3]
    bg = pd_ref[0, :, 2 * W_GRP:3 * W_GRP].astype(F32)
    g = pd_ref[0, :, 3 * W_GRP:].astype(F32)
    od_ref[0] = (bg * y * _silu(g)).astype(BF16)


def _local_mixers(pa, pd, sgu_w, sgu_b, conv_w):
    b, t, _ = pa.shape
    nblk = t // TM
    hb = TM // 16
    return pl.pallas_call(
        functools.partial(_local_kernel, nblk),
        grid=(b, nblk),
        in_specs=[pl.BlockSpec((1, TM, N_A), lambda bb, i: (bb, i, 0)),
                  pl.BlockSpec((1, TM, N_D), lambda bb, i: (bb, i, 0)),
                  pl.BlockSpec((1, 16, 2 * W_GRP), lambda bb, i: (bb, jnp.maximum(i * hb - 1, 0), 0)),
                  pl.BlockSpec((1, 16, 2 * W_GRP),
                               lambda bb, i: (bb, jnp.minimum((i + 1) * hb, nblk * hb - 1), 0)),
                  pl.BlockSpec(sgu_w.shape, lambda bb, i: (0, 0, 0)),
                  pl.BlockSpec((CHUNK_A, sgu_w.shape[0]), lambda bb, i: (0, 0)),
                  pl.BlockSpec(conv_w.shape, lambda bb, i: (0, 0))],
        out_specs=[pl.BlockSpec((1, TM, W_GRP), lambda bb, i: (bb, i, 0))] * 2,
        out_shape=[jax.ShapeDtypeStruct((b, t, W_GRP), BF16)] * 2,
        compiler_params=_params(("parallel", "parallel")),
        name="local_mixers",
    )(pa, pd, pd, pd, sgu_w, sgu_b.T, conv_w)


def _attn_kernel(lam_init, nsub, *refs):
    q_refs, (k_ref, v_ref), g_refs = refs[:nsub], refs[nsub:nsub + 2], refs[nsub + 2:2 * nsub + 2]
    lam_ref, sg_ref, o_ref, vext_ref = refs[2 * nsub + 2:]
    tq = nsub * TM
    n_keys = k_ref.shape[1]

    @pl.when(pl.program_id(2) == 0)
    def _():
        vext_ref[:, :HD_B] = v_ref[0]
        vext_ref[:, HD_B:] = jnp.ones((n_keys, HD_B), BF16)

    lp = lam_ref[...]
    lam = (jnp.exp(jnp.sum(lp[0:1] * lp[1:2], axis=-1, keepdims=True))
           - jnp.exp(jnp.sum(lp[2:3] * lp[3:4], axis=-1, keepdims=True)) + lam_init)
    q = jnp.concatenate([r[0] for r in q_refs], axis=0)
    lane = lax.broadcasted_iota(jnp.int32, (1, HD_B), 1)
    zero = jnp.zeros_like(q)
    q2 = jnp.concatenate([jnp.where(lane < HD_QK, q, zero), jnp.where(lane >= HD_QK, q, zero)], axis=0)
    m = o = None
    for k0 in range(0, n_keys, ATT_KB):
        k1 = min(k0 + ATT_KB, n_keys)
        s = _mm_nt(q2, k_ref[0, k0:k1])
        bm = jnp.max(s, axis=-1, keepdims=True)
        if m is None:
            m = bm
            o = _mm(jnp.exp2(s - m).astype(BF16), vext_ref[k0:k1])
        else:
            m_new = jnp.maximum(m, bm)
            o = o * jnp.exp2(m - m_new) + _mm(jnp.exp2(s - m_new).astype(BF16), vext_ref[k0:k1])
            m = m_new
    a = o[:tq, :HD_B] / o[:tq, HD_B:] - lam * (o[tq:, :HD_B] / o[tq:, HD_B:])
    y = a * lax.rsqrt(jnp.mean(a * a, axis=-1, keepdims=True) + EPS) * sg_ref[...] * (1.0 - lam_init)
    g = jnp.concatenate([r[0] for r in g_refs], axis=0).astype(F32)
    o_ref[0] = (y * _silu(g)).astype(BF16)


def _attention(pb, lam_params, subln_g, lam_init, q_blk0, n_q, n_keys, nsub):
    b = pb.shape[0]
    nh = W_GRP // HD_B
    q_spec = lambda col0, j: pl.BlockSpec(
        (1, TM, HD_B), lambda bb, h, i: (bb, nsub * i + q_blk0 + j, col0 + h))
    kv_spec = lambda col0: pl.BlockSpec((1, n_keys, HD_B), lambda bb, h, i: (bb, 0, col0 + h))
    return pl.pallas_call(
        functools.partial(_attn_kernel, lam_init, nsub),
        grid=(b, nh, n_q // (nsub * TM)),
        in_specs=([q_spec(0, j) for j in range(nsub)] + [kv_spec(nh), kv_spec(2 * nh)]
                  + [q_spec(3 * nh, j) for j in range(nsub)]
                  + [pl.BlockSpec((4, HD_QK), lambda bb, h, i: (0, 0)),
                     pl.BlockSpec((1, HD_B), lambda bb, h, i: (0, 0))]),
        out_specs=pl.BlockSpec((1, nsub * TM, HD_B), lambda bb, h, i: (bb, i, h)),
        out_shape=jax.ShapeDtypeStruct((b, n_q, W_GRP), BF16),
        scratch_shapes=[pltpu.VMEM((n_keys, 2 * HD_B), BF16)],
        compiler_params=_params(("parallel", "parallel", "arbitrary")),
        name="diff_attention",
    )(*([pb] * (2 * nsub + 2)), lam_params, subln_g.reshape(1, HD_B))


def _seg_sum(x, ones_bd):
    hi, lo = _split_bf16(x)
    return _mm(hi, ones_bd) + _mm(lo, ones_bd)


def _head_diag(x4, mask):
    return jnp.where(mask, x4, 0.0).astype(BF16)


def _rwkv_prep_kernel(n_blk, rkv_ref, lo_ref, hrkv_ref, hlo_ref, mu_ref, mul_ref, w0_ref, w2_ref,
                      a0_ref, a2_ref, kk_ref, ka_ref, rk_ref,
                      g_ref, q_ref, rp_ref, yp_ref, bonus_ref):
    dr = pl.program_id(1)
    i = pl.program_id(2)
    fwd = dr == 0
    nl = RW_G // HD_C
    sgn = 1 - 2 * dr
    hrow = jnp.where(fwd, 15, 0)
    edge = jnp.logical_or(i == 0, i == jnp.where(fwd, 1, n_blk - 1))
    rsel = lax.broadcasted_iota(jnp.int32, (TM, 1), 0) == jnp.where(fwd, 0, TM - 1)
    hsel = lax.broadcasted_iota(jnp.int32, (16, 1), 0) == hrow

    def shifted(cur_ref, halo_ref, mu):
        f = cur_ref[0].astype(F32)
        halo = jnp.sum(jnp.where(hsel, halo_ref[0].astype(F32), 0.0), axis=0, keepdims=True)
        halo = jnp.where(edge, 0.0, halo)
        rolled = jnp.where(fwd, pltpu.roll(f, 1, axis=0), pltpu.roll(f, TM - 1, axis=0))
        prev = jnp.where(rsel, halo, rolled)
        return f + (prev - f) * mu

    z = shifted(rkv_ref, hrkv_ref, mu_ref[0])
    zl = shifted(lo_ref, hlo_ref, mul_ref[0])
    r, k, v = z[:, :W_GRP], z[:, W_GRP:2 * W_GRP], z[:, 2 * W_GRP:]
    wl = w0_ref[0] + _mm(jnp.tanh(zl).astype(BF16), w2_ref[0].astype(BF16))
    logd = -math.exp(-0.5) * _sigmoid(wl)
    a = _sigmoid(a0_ref[0] + _mm(zl.astype(BF16), a2_ref[0].astype(BF16)))
    ri = lax.broadcasted_iota(jnp.int32, (W_GRP, W_GRP), 0) // HD_C
    ci = lax.broadcasted_iota(jnp.int32, (W_GRP, W_GRP), 1) // HD_C
    ones_bd = jnp.where(ri == ci, 1.0, 0.0).astype(BF16)
    kk = k * kk_ref[0]
    kk = kk / jnp.maximum(jnp.sqrt(_seg_sum(kk * kk, ones_bd)), 1e-12)
    k2 = k * (1.0 + (a - 1.0) * ka_ref[0])
    bv = kk * a
    bonus_ref[0, 0] = _seg_sum(r * k2 * rk_ref[0], ones_bd) * v
    rt = lax.broadcasted_iota(jnp.int32, (TM, TM), 0)
    ct = lax.broadcasted_iota(jnp.int32, (TM, TM), 1)
    same = rt // RW_L == ct // RW_L
    tri = jnp.where(jnp.logical_and(same, (rt - ct) * sgn >= 0), 1.0, 0.0).astype(BF16)
    blk = jnp.where(same, 1.0, 0.0).astype(BF16)
    lh, ll = _split_bf16(logd)
    cum = _mm(tri, lh) + _mm(tri, ll)
    tot = _mm(blk, lh) + _mm(blk, ll)
    half = 0.5 * tot
    e_pos = jnp.exp(cum - half)
    e_neg = jnp.exp(half - cum)
    e_prev = jnp.exp(cum - logd - half)
    e_end = jnp.exp(tot - cum)
    g_end = jnp.exp(tot)
    e_half = jnp.exp(half)
    kt_all = kk * e_prev
    rt_all = r * e_pos
    kh_all = k2 * e_neg
    bh_all = bv * e_neg
    kb_all = k2 * e_end
    bb_all = bv * e_end
    row = lax.broadcasted_iota(jnp.int32, (RW_L, RW_G), 0)
    col = lax.broadcasted_iota(jnp.int32, (RW_L, RW_G), 1) % HD_C
    eye = jnp.where(col == row, 1.0, 0.0)
    r4 = lax.broadcasted_iota(jnp.int32, (RW_G, RW_G), 0) // HD_C
    c4 = lax.broadcasted_iota(jnp.int32, (RW_G, RW_G), 1) // HD_C
    dmask = r4 == c4
    r2 = lax.broadcasted_iota(jnp.int32, (2 * RW_L, RW_G), 0) % RW_L
    c2 = lax.broadcasted_iota(jnp.int32, (2 * RW_L, RW_G), 1) % HD_C
    first = jnp.where(lax.broadcasted_iota(jnp.int32, (2 * RW_L, RW_G), 0) < RW_L, 1, 0)
    mask2 = (r2 - c2) * sgn >= first

    def bd(x):
        return _head_diag(jnp.concatenate([x] * nl, axis=0), dmask)

    def hprod(x, y):
        return _mm(x.astype(BF16), bd(y))

    def diag_blocks(full):
        out = jnp.zeros((RW_L, RW_G), F32)
        lane_h = lax.broadcasted_iota(jnp.int32, (RW_L, RW_G), 1) // HD_C
        for hh in range(nl):
            out = out + jnp.where(lane_h == hh, full[hh * HD_C:(hh + 1) * HD_C, :], 0.0)
        return out

    pairs = [(slice(c * RW_L, (c + 1) * RW_L), slice(gi * RW_G, (gi + 1) * RW_G))
             for c in range(TM // RW_L) for gi in range(W_GRP // RW_G)]
    cat = lambda *xs: jnp.concatenate(xs, axis=0)
    L = RW_L
    kt = [kt_all[p] for p in pairs]
    rtl = [rt_all[p] for p in pairs]
    vv = [v[p] for p in pairs]
    lhs = [cat(a_, b_).astype(BF16) for a_, b_ in zip(kt, rtl)]
    pb_ = [jnp.where(mask2, _mm_nt(x, bd(bh_all[p])), 0.0) for x, p in zip(lhs, pairs)]
    pk_ = [jnp.where(mask2, _mm_nt(x, bd(kh_all[p])), 0.0) for x, p in zip(lhs, pairs)]
    res = [hprod(x, x[:L]) for x in pb_]
    tinv = [eye - x[:L] for x in pb_]
    mt = [x[L:] - y[L:] for x, y in zip(pb_, res)]
    pw = [y[:L] for y in res]
    for _ in range(4):
        res = [hprod(cat(t_, m_, w_), w_) for t_, m_, w_ in zip(tinv, mt, pw)]
        tinv = [t_ + y[:L] for t_, y in zip(tinv, res)]
        mt = [m_ + y[L:2 * L] for m_, y in zip(mt, res)]
        pw = [y[2 * L:] for y in res]
    res = [hprod(cat(t_, m_), w_) for t_, m_, w_ in zip(tinv, mt, pw)]
    tm = [cat(t_, m_) + y for t_, m_, y in zip(tinv, mt, res)]
    wv = [hprod(x, y) for x, y in zip(pk_, vv)]
    tmb = [x.astype(BF16) for x in tm]
    ww = [_mm(x, bd(y)) for x, y in zip(tmb, kt)]
    uu = [_mm(x, bd(y[:L])) for x, y in zip(tmb, wv)]
    for j, p in enumerate(pairs):
        rows, cols = p
        eh = e_half[rows.start:rows.start + 1, cols]
        rp_ref[0, 0, rows, cols] = (rtl[j] - ww[j][L:]) * eh
        yp_ref[0, 0, rows, cols] = wv[j][L:] - uu[j][L:]
        xt = cat(kb_all[p], -bb_all[p]).T.astype(BF16)
        zeros = jnp.zeros((L, RW_G), F32)
        yq = cat(vv[j], uu[j][:L]).astype(BF16)
        yg = cat(zeros, ww[j][:L]).astype(BF16)
        q_ref[0, 0, rows, cols] = diag_blocks(_mm(xt, yq))
        g_ref[0, 0, rows, cols] = (diag_blocks(_mm(xt, yg)) * eh
                                   + eye * g_end[rows.start:rows.start + 1, cols])


def _rwkv_scan_kernel(n_blk, g_ref, q_ref, rp_ref, yp_ref, y_ref, h_ref):
    nl = RW_G // HD_C

    @pl.when(pl.program_id(2) == 0)
    def _():
        h_ref[...] = jnp.zeros_like(h_ref)

    dr = pl.program_id(1)
    r4 = lax.broadcasted_iota(jnp.int32, (RW_G, RW_G), 0) // HD_C
    c4 = lax.broadcasted_iota(jnp.int32, (RW_G, RW_G), 1) // HD_C
    dmask = r4 == c4
    n_chunk = TM // RW_L
    for s in range(n_chunk):
        c = jnp.where(dr == 0, s, n_chunk - 1 - s)
        rows = pl.ds(pl.multiple_of(c * RW_L, RW_L), RW_L)
        for gi in range(W_GRP // RW_G):
            cols = slice(gi * RW_G, (gi + 1) * RW_G)
            h = h_ref[:, cols]
            h4 = jnp.where(dmask, jnp.concatenate([h] * nl, axis=0), 0.0)
            hh, hl = _split_bf16(h4)
            lhs = jnp.concatenate([g_ref[0, 0, rows, cols], rp_ref[0, 0, rows, cols]], axis=0)
            lh, ll = _split_bf16(lhs)
            prod = _mm(lh, hh) + _mm(lh, hl) + _mm(ll, hh)
            y_ref[0, 0, rows, cols] = prod[RW_L:] + yp_ref[0, 0, rows, cols]
            h_ref[:, cols] = prod[:RW_L] + q_ref[0, 0, rows, cols]


def _rwkv_merge_kernel(y_ref, bonus_ref, g_ref, lw_ref, lb_ref, o_ref):
    ri = lax.broadcasted_iota(jnp.int32, (W_GRP, W_GRP), 0) // HD_C
    ci = lax.broadcasted_iota(jnp.int32, (W_GRP, W_GRP), 1) // HD_C
    ones_bd = jnp.where(ri == ci, 1.0, 0.0).astype(BF16)
    y = y_ref[0, 0] + y_ref[0, 1]
    mean = _seg_sum(y, ones_bd) * (1.0 / HD_C)
    yc = y - mean
    var = _seg_sum(yc * yc, ones_bd) * (1.0 / HD_C)
    yn = yc * lax.rsqrt(var + GN_EPS) * lw_ref[...] + lb_ref[...]
    out = yn + bonus_ref[0, 0] + bonus_ref[0, 1]
    o_ref[0] = (out * _silu(g_ref[0].astype(F32))).astype(BF16)


def _scan_block(dr, i, n_blk):
    return jnp.where(dr == 0, i, jnp.where(i == 0, 0, n_blk - i))


def _rwkv(pc, p):
    b, t, _ = pc.shape
    nblk = t // TM
    hb = TM // 16
    lblk = 4 * W_GRP // LORA_PAD

    def halo_idx(dr, i):
        return jnp.where(dr == 0, jnp.maximum(i * hb - 1, 0), jnp.minimum((i + 1) * hb, nblk * hb - 1))

    per_dir = lambda n: pl.BlockSpec((1, 1, n), lambda bb, dr, i: (dr, 0, 0))
    per_dir_m = lambda m, n: pl.BlockSpec((1, m, n), lambda bb, dr, i: (dr, 0, 0))
    state = lambda: pl.BlockSpec((1, 1, TM, W_GRP), lambda bb, dr, i: (bb, dr, i, 0))
    sds = jax.ShapeDtypeStruct((b, 2, t, W_GRP), F32)
    gm, qm, rp, yp, bonus = pl.pallas_call(
        functools.partial(_rwkv_prep_kernel, nblk),
        grid=(b, 2, nblk),
        in_specs=[pl.BlockSpec((1, TM, 3 * W_GRP), lambda bb, dr, i: (bb, i, 0)),
                  pl.BlockSpec((1, TM, LORA_PAD), lambda bb, dr, i: (bb, i, lblk + dr)),
                  pl.BlockSpec((1, 16, 3 * W_GRP), lambda bb, dr, i: (bb, halo_idx(dr, i), 0)),
                  pl.BlockSpec((1, 16, LORA_PAD), lambda bb, dr, i: (bb, halo_idx(dr, i), lblk + dr)),
                  per_dir(3 * W_GRP), per_dir(LORA_PAD), per_dir(W_GRP), per_dir_m(LORA_PAD, W_GRP),
                  per_dir(W_GRP), per_dir_m(LORA_PAD, W_GRP), per_dir(W_GRP), per_dir(W_GRP),
                  per_dir(W_GRP)],
        out_specs=[state()] * 5,
        out_shape=[sds] * 5,
        compiler_params=_params(("parallel", "parallel", "parallel")),
        name="rwkv_prep",
    )(pc, pc, pc, pc, p["mu_rkv"], p["mu_lora"], p["w0"], p["w2"], p["a0"], p["a2"],
      p["kk"], p["ka"], p["rk"])

    scan_spec = lambda: pl.BlockSpec((1, 1, TM, W_GRP),
                                     lambda bb, dr, i: (bb, dr, _scan_block(dr, i, nblk), 0))
    y = pl.pallas_call(
        functools.partial(_rwkv_scan_kernel, nblk),
        grid=(b, 2, nblk),
        in_specs=[scan_spec()] * 4,
        out_specs=scan_spec(),
        out_shape=sds,
        scratch_shapes=[pltpu.VMEM((RW_L, W_GRP), F32)],
        compiler_params=_params(("parallel", "parallel", "arbitrary")),
        name="rwkv_scan",
    )(gm, qm, rp, yp)

    return pl.pallas_call(
        _rwkv_merge_kernel,
        grid=(b, nblk),
        in_specs=[pl.BlockSpec((1, 2, TM, W_GRP), lambda bb, i: (bb, 0, i, 0)),
                  pl.BlockSpec((1, 2, TM, W_GRP), lambda bb, i: (bb, 0, i, 0)),
                  pl.BlockSpec((1, TM, W_GRP), lambda bb, i: (bb, i, 3)),
                  pl.BlockSpec((1, W_GRP), lambda bb, i: (0, 0)),
                  pl.BlockSpec((1, W_GRP), lambda bb, i: (0, 0))],
        out_specs=pl.BlockSpec((1, TM, W_GRP), lambda bb, i: (bb, i, 0)),
        out_shape=jax.ShapeDtypeStruct((b, t, W_GRP), BF16),
        compiler_params=_params(("parallel", "parallel")),
        name="rwkv_merge",
    )(y, bonus, pc, p["ln_w"], p["ln_b"])


def _outproj_kernel(mod_row, oa_ref, ob_ref, oc_ref, od_ref, w_ref, x_ref, mod_ref, g_ref, o_ref):
    d = x_ref.shape[-1]
    b = pl.program_id(0)
    acc = _mm(oa_ref[0], w_ref[0:W_GRP, :])
    acc = acc + _mm(ob_ref[0], w_ref[W_GRP:2 * W_GRP, :])
    acc = acc + _mm(oc_ref[0], w_ref[2 * W_GRP:3 * W_GRP, :])
    acc = acc + _mm(od_ref[0], w_ref[3 * W_GRP:, :])
    y = acc * lax.rsqrt(jnp.mean(acc * acc, axis=-1, keepdims=True) + EPS) * g_ref[...]
    row = mod_ref[0, pl.ds(b if mod_row is None else mod_row, 1), :]
    o_ref[0] = x_ref[0] + row[:, 2 * d:] * y


def _outproj(mix, w_out, x, mods, layer, g_post, blk0, mod_row):
    b, n, d = x.shape
    o_spec = lambda off: pl.BlockSpec((1, TM, W_GRP), lambda bb, i: (bb, i + off, 0))
    return pl.pallas_call(
        functools.partial(_outproj_kernel, mod_row),
        grid=(b, n // TM),
        in_specs=[o_spec(off) for off in blk0] + [
            pl.BlockSpec(w_out.shape, lambda bb, i: (0, 0)),
            pl.BlockSpec((1, TM, d), lambda bb, i: (bb, i, 0)),
            pl.BlockSpec((1, 8, 3 * d), lambda bb, i: (layer, 0, 0)),
            pl.BlockSpec((1, d), lambda bb, i: (0, 0))],
        out_specs=pl.BlockSpec((1, TM, d), lambda bb, i: (bb, i, 0)),
        out_shape=jax.ShapeDtypeStruct((b, n, d), F32),
        compiler_params=_params(("parallel", "parallel")),
        name="outproj",
    )(*mix, w_out, x, mods, g_post.reshape(1, d))


def _regroup_w_in(w):
    g = W_GRP
    o = [0]
    for s in (g, g, g, g, g, g, g, 3 * g, LORA_W + LORA_A, LORA_W + LORA_A, g, g, g, g, g):
        o.append(o[-1] + s)
    col = lambda j: w[:, o[j]:o[j + 1]]
    pad = jnp.zeros((w.shape[0], LORA_PAD - LORA_W - LORA_A), w.dtype)
    parts = [col(0), col(1), col(2), col(3), col(4), col(5), col(6),
             col(12), col(13), col(11), col(14),
             col(7), col(10), col(8), pad, col(9), pad]
    return jnp.concatenate(parts, axis=1).astype(BF16)


def _rope_tables(n_ctx, n_lat):
    pos = jnp.arange(n_lat)
    inv = ROPE_BASE ** (-jnp.arange(0, 32, 2, dtype=F32) / 32)
    lane = jnp.arange(HD_B)
    use_col = (lane // 32) % 2 == 1
    p = jnp.where(use_col[None, :], (pos % GRID_W)[:, None], (pos // GRID_W)[:, None]).astype(F32)
    ang = p * inv[lane % 16][None, :]
    first = (lane % 32 < 16)[None, :]
    cos, sin = jnp.cos(ang), jnp.sin(ang)
    sa = jnp.where(first, -sin, 0.0)
    sb = jnp.where(first, 0.0, sin)
    ident = lambda v: jnp.full((n_ctx, HD_B), v, F32)
    return (jnp.concatenate([ident(1.0), cos]), jnp.concatenate([ident(0.0), sa]),
            jnp.concatenate([ident(0.0), sb]))


def _rwkv_params(l, mu, w0, w2, a0, a2, kk, ka, rk, ln_w, ln_b):
    g = W_GRP
    zpad = lambda n: jnp.zeros((2, n), F32)
    mu_l = jnp.concatenate([mu[l][:, 3 * g:], zpad(LORA_PAD - LORA_W - LORA_A)], axis=1)
    w2p = jnp.concatenate([w2[l], jnp.zeros((2, LORA_PAD - LORA_W, g), F32)], axis=1)
    a2p = jnp.concatenate([jnp.zeros((2, LORA_W, g), F32), a2[l],
                           jnp.zeros((2, LORA_PAD - LORA_W - LORA_A, g), F32)], axis=1)
    r3 = lambda v: v.reshape(2, 1, -1)
    return dict(mu_rkv=r3(mu[l][:, :3 * g]), mu_lora=r3(mu_l), w0=r3(w0[l]), w2=w2p, a0=r3(a0[l]),
                a2=a2p, kk=r3(kk[l]), ka=r3(ka[l]), rk=r3(rk[l].reshape(2, g)),
                ln_w=ln_w[l].reshape(1, g), ln_b=ln_b[l].reshape(1, g))


def kernel(x, c, ctx, c_ctx, w_mod, b_mod, g_pre, g_post, w_in, w_out, sgu_w, sgu_b, lam_q1, lam_k1, lam_q2, lam_k2, subln_g, rwkv_mu, rwkv_w0, rwkv_w2, rwkv_a0, rwkv_a2, rwkv_kk, rwkv_ka, rwkv_rk, rwkv_ln_w, rwkv_ln_b, conv_w):
    depth = w_mod.shape[0]
    b, n_lat, _ = x.shape
    n_ctx = ctx.shape[1]
    mods = _modulation(c, c_ctx, w_mod, b_mod)
    rope = _rope_tables(n_ctx, n_lat)
    xc = ctx
    for l in range(depth):
        need_ctx = l < depth - 1
        lam_init = 0.8 - 0.6 * math.exp(-0.3 * l)
        pa, pb, pd, pc = _inproj(xc, x, mods, l, g_pre[l], _regroup_w_in(w_in[l]), rope)
        o_a, o_d = _local_mixers(pa, pd, sgu_w[l], sgu_b[l], conv_w[l])
        lam_params = jnp.stack([lam_q1[l], lam_k1[l], lam_q2[l], lam_k2[l]])
        cb = n_ctx // TM
        o_b = _attention(pb, lam_params, subln_g[l], lam_init, cb, n_lat, n_ctx + n_lat, 2)
        o_c = _rwkv(pc, _rwkv_params(l, rwkv_mu, rwkv_w0, rwkv_w2, rwkv_a0, rwkv_a2, rwkv_kk,
                                     rwkv_ka, rwkv_rk, rwkv_ln_w, rwkv_ln_b))
        w_o = w_out[l].astype(BF16)
        x_new = _outproj((o_a, o_b, o_c, o_d), w_o, x, mods, l, g_post[l], (cb, 0, cb, cb), None)
        if need_ctx:
            o_bc = _attention(pb, lam_params, subln_g[l], lam_init, 0, n_ctx, n_ctx, 1)
            xc = _outproj((o_a, o_bc, o_c, o_d), w_o, xc, mods, l, g_post[l], (0, 0, 0, 0), b)
        x = x_new
    return x
```

```python
import functools
import math

import jax
import jax.numpy as jnp
from jax import lax
from jax.experimental import pallas as pl
from jax.experimental.pallas import tpu as pltpu

F32 = jnp.float32
BF16 = jnp.bfloat16

EPS = 1e-6
GN_EPS = 64e-5
GRID_W = 64
ROPE_BASE = 10000.0
LOG2E = 1.4426950408889634

W_GRP = 512
CHUNK_A = 128
HD_B = 128
HD_QK = 64
HD_C = 64
LORA_W = 64
LORA_A = 32
LORA_PAD = 128
RW_L = 64
RW_G = 256
TM = 256
ATT_KB = 1024
VMEM_LIMIT = 56 * 1024 * 1024

N_A = 3 * W_GRP
N_B = 4 * W_GRP
N_D = 4 * W_GRP
N_C = 4 * W_GRP + 2 * LORA_PAD
OFF_A, OFF_B, OFF_D, OFF_C = 0, N_A, N_A + N_B, N_A + N_B + N_D
N_ALL = OFF_C + N_C


def _silu(g):
    return g * (1.0 / (1.0 + jnp.exp(-g)))


def _sigmoid(x):
    return 1.0 / (1.0 + jnp.exp(-x))


def _split_bf16(x):
    hi = x.astype(BF16)
    lo = (x - hi.astype(F32)).astype(BF16)
    return hi, lo


def _mm(a, b):
    return jnp.dot(a, b, preferred_element_type=F32)


def _mm_nt(a, b):
    return lax.dot_general(a, b, (((1,), (1,)), ((), ())), preferred_element_type=F32)


def _params(sem):
    return pltpu.CompilerParams(dimension_semantics=sem, vmem_limit_bytes=VMEM_LIMIT)


def _mod_kernel(s_ref, w_ref, b_ref, o_ref):
    s = _silu(s_ref[...])
    o_ref[0] = _mm(s.astype(BF16), w_ref[0].astype(BF16)) + b_ref[0]


def _modulation(c, c_ctx, w_mod, b_mod):
    depth, d, n3 = w_mod.shape
    b = c.shape[0]
    rows = jnp.zeros((8, d), F32).at[:b].set(c).at[b].set(c_ctx)
    tn = 768
    return pl.pallas_call(
        _mod_kernel,
        grid=(depth, n3 // tn),
        in_specs=[pl.BlockSpec((8, d), lambda l, j: (0, 0)),
                  pl.BlockSpec((1, d, tn), lambda l, j: (l, 0, j)),
                  pl.BlockSpec((1, 1, tn), lambda l, j: (l, 0, j))],
        out_specs=pl.BlockSpec((1, 8, tn), lambda l, j: (l, 0, j)),
        out_shape=jax.ShapeDtypeStruct((depth, 8, n3), F32),
        compiler_params=_params(("parallel", "parallel")),
        name="modulation",
    )(rows, w_mod, b_mod.reshape(depth, 1, n3))


def _inproj_kernel(n_batch, xc_ref, xl_ref, mod_ref, g_ref, w_ref, cos_ref, sa_ref, sb_ref,
                   pa_ref, pb_ref, pd_ref, pc_ref):
    b = pl.program_id(0)
    i = pl.program_id(1)
    d = xc_ref.shape[-1]
    is_ctx = i == 0
    x = jnp.where(is_ctx, xc_ref[0], xl_ref[0])
    y = x * lax.rsqrt(jnp.mean(x * x, axis=-1, keepdims=True) + EPS) * g_ref[...]
    row = jnp.where(is_ctx, mod_ref[0, pl.ds(n_batch, 1), :], mod_ref[0, pl.ds(b, 1), :])
    h = (y * (1.0 + row[:, d:2 * d]) + row[:, :d]).astype(BF16)

    def proj(c0, c1):
        return _mm(h, w_ref[0, :, c0:c1])

    pa_ref[0] = proj(OFF_A, OFF_A + N_A).astype(BF16)
    cos, sa, sb = cos_ref[...], sa_ref[...], sb_ref[...]
    qk = proj(OFF_B, OFF_B + 2 * W_GRP)
    for j in range(2 * W_GRP // HD_B):
        t = qk[:, j * HD_B:(j + 1) * HD_B]
        t = t * cos + pltpu.roll(t, HD_B - 16, axis=1) * sa + pltpu.roll(t, 16, axis=1) * sb
        if j < W_GRP // HD_B:
            t = t * (HD_QK ** -0.5 * LOG2E)
        pb_ref[0, :, j * HD_B:(j + 1) * HD_B] = t.astype(BF16)
    pb_ref[0, :, 2 * W_GRP:] = proj(OFF_B + 2 * W_GRP, OFF_B + N_B).astype(BF16)
    pd_ref[0] = proj(OFF_D, OFF_D + N_D).astype(BF16)
    pc_ref[0] = proj(OFF_C, OFF_C + N_C).astype(BF16)


def _inproj(xc, xl, mods, layer, g_pre, w_in, rope):
    b, n_ctx, d = xc.shape
    n_lat = xl.shape[1]
    t = n_ctx + n_lat
    assert n_ctx == TM and n_lat % TM == 0
    nblk = t // TM
    cos, sa, sb = rope
    tab = pl.BlockSpec((TM, HD_B), lambda bb, i: (i, 0))
    out = lambda n: pl.BlockSpec((1, TM, n), lambda bb, i: (bb, i, 0))
    return pl.pallas_call(
        functools.partial(_inproj_kernel, b),
        grid=(b, nblk),
        in_specs=[pl.BlockSpec((1, TM, d), lambda bb, i: (bb, 0, 0)),
                  pl.BlockSpec((1, TM, d), lambda bb, i: (bb, jnp.maximum(i - 1, 0), 0)),
                  pl.BlockSpec((1, 8, 3 * d), lambda bb, i: (layer, 0, 0)),
                  pl.BlockSpec((1, d), lambda bb, i: (0, 0)),
                  pl.BlockSpec((1, d, N_ALL), lambda bb, i: (layer, 0, 0), pipeline_mode=pl.Buffered(1)),
                  tab, tab, tab],
        out_specs=[out(N_A), out(N_B), out(N_D), out(N_C)],
        out_shape=[jax.ShapeDtypeStruct((b, t, n), BF16) for n in (N_A, N_B, N_D, N_C)],
        compiler_params=_params(("parallel", "arbitrary")),
        name="inproj",
    )(xc, xl, mods, g_pre.reshape(1, d), w_in, cos, sa, sb)


def _local_kernel(n_blk, pa_ref, pd_ref, hp_ref, hn_ref, ws_ref, bs_ref, cw_ref, oa_ref, od_ref):
    i = pl.program_id(1)
    for c in range(TM // CHUNK_A):
        rows = slice(c * CHUNK_A, (c + 1) * CHUNK_A)
        for h in range(W_GRP // CHUNK_A):
            cols = slice(h * CHUNK_A, (h + 1) * CHUNK_A)
            u = pa_ref[0, rows, cols].astype(F32)
            v = pa_ref[0, rows, W_GRP + h * CHUNK_A:W_GRP + (h + 1) * CHUNK_A].astype(F32)
            g = pa_ref[0, rows, 2 * W_GRP + h * CHUNK_A:2 * W_GRP + (h + 1) * CHUNK_A].astype(F32)
            mean = jnp.mean(v, axis=-1, keepdims=True)
            vc = v - mean
            vn = vc * lax.rsqrt(jnp.mean(vc * vc, axis=-1, keepdims=True) + EPS)
            mixed = _mm(ws_ref[h].astype(BF16), vn.astype(BF16)) + bs_ref[:, h:h + 1]
            oa_ref[0, rows, cols] = (u * mixed * _silu(g)).astype(BF16)
    cx = pd_ref[0, :, 0:2 * W_GRP].astype(F32)
    z = cx[:, :W_GRP] * cx[:, W_GRP:]
    hp = hp_ref[0].astype(F32)
    hn = hn_ref[0].astype(F32)
    zp = hp[15:16, :W_GRP] * hp[15:16, W_GRP:]
    zn = hn[0:1, :W_GRP] * hn[0:1, W_GRP:]
    zp = jnp.where(i <= 1, 0.0, zp)
    zn = jnp.where(jnp.logical_or(i == 0, i == n_blk - 1), 0.0, zn)
    r = lax.broadcasted_iota(jnp.int32, (TM, 1), 0)
    z_prev = jnp.where(r == 0, zp, pltpu.roll(z, 1, axis=0))
    z_next = jnp.where(r == TM - 1, zn, pltpu.roll(z, TM - 1, axis=0))
    w = cw_ref[...]
    y = z_prev * w[0:1] + z * w[1:2] + z_next * w[2:3]
    bg = pd_ref[0, :, 2 * W_GRP:3 * W_GRP].astype(F32)
    g = pd_ref[0, :, 3 * W_GRP:].astype(F32)
    od_ref[0] = (bg * y * _silu(g)).astype(BF16)


def _local_mixers(pa, pd, sgu_w, sgu_b, conv_w):
    b, t, _ = pa.shape
    nblk = t // TM
    hb = TM // 16
    return pl.pallas_call(
        functools.partial(_local_kernel, nblk),
        grid=(b, nblk),
        in_specs=[pl.BlockSpec((1, TM, N_A), lambda bb, i: (bb, i, 0)),
                  pl.BlockSpec((1, TM, N_D), lambda bb, i: (bb, i, 0)),
                  pl.BlockSpec((1, 16, 2 * W_GRP), lambda bb, i: (bb, jnp.maximum(i * hb - 1, 0), 0)),
                  pl.BlockSpec((1, 16, 2 * W_GRP),
                               lambda bb, i: (bb, jnp.minimum((i + 1) * hb, nblk * hb - 1), 0)),
                  pl.BlockSpec(sgu_w.shape, lambda bb, i: (0, 0, 0)),
                  pl.BlockSpec((CHUNK_A, sgu_w.shape[0]), lambda bb, i: (0, 0)),
                  pl.BlockSpec(conv_w.shape, lambda bb, i: (0, 0))],
        out_specs=[pl.BlockSpec((1, TM, W_GRP), lambda bb, i: (bb, i, 0))] * 2,
        out_shape=[jax.ShapeDtypeStruct((b, t, W_GRP), BF16)] * 2,
        compiler_params=_params(("parallel", "parallel")),
        name="local_mixers",
    )(pa, pd, pd, pd, sgu_w, sgu_b.T, conv_w)


def _attn_kernel(lam_init, nsub, *refs):
    q_refs, (k_ref, v_ref), g_refs = refs[:nsub], refs[nsub:nsub + 2], refs[nsub + 2:2 * nsub + 2]
    lam_ref, sg_ref, o_ref, vext_ref = refs[2 * nsub + 2:]
    tq = nsub * TM
    n_keys = k_ref.shape[1]

    @pl.when(pl.program_id(2) == 0)
    def _():
        vext_ref[:, :HD_B] = v_ref[0]
        vext_ref[:, HD_B:] = jnp.ones((n_keys, HD_B), BF16)

    lp = lam_ref[...]
    lam = (jnp.exp(jnp.sum(lp[0:1] * lp[1:2], axis=-1, keepdims=True))
           - jnp.exp(jnp.sum(lp[2:3] * lp[3:4], axis=-1, keepdims=True)) + lam_init)
    q = jnp.concatenate([r[0] for r in q_refs], axis=0)
    lane = lax.broadcasted_iota(jnp.int32, (1, HD_B), 1)
    zero = jnp.zeros_like(q)
    q2 = jnp.concatenate([jnp.where(lane < HD_QK, q, zero), jnp.where(lane >= HD_QK, q, zero)], axis=0)
    m = o = None
    for k0 in range(0, n_keys, ATT_KB):
        k1 = min(k0 + ATT_KB, n_keys)
        s = _mm_nt(q2, k_ref[0, k0:k1])
        bm = jnp.max(s, axis=-1, keepdims=True)
        if m is None:
            m = bm
            o = _mm(jnp.exp2(s - m).astype(BF16), vext_ref[k0:k1])
        else:
            m_new = jnp.maximum(m, bm)
            o = o * jnp.exp2(m - m_new) + _mm(jnp.exp2(s - m_new).astype(BF16), vext_ref[k0:k1])
            m = m_new
    a = o[:tq, :HD_B] / o[:tq, HD_B:] - lam * (o[tq:, :HD_B] / o[tq:, HD_B:])
    y = a * lax.rsqrt(jnp.mean(a * a, axis=-1, keepdims=True) + EPS) * sg_ref[...] * (1.0 - lam_init)
    g = jnp.concatenate([r[0] for r in g_refs], axis=0).astype(F32)
    o_ref[0] = (y * _silu(g)).astype(BF16)


def _attention(pb, lam_params, subln_g, lam_init, q_blk0, n_q, n_keys, nsub):
    b = pb.shape[0]
    nh = W_GRP // HD_B
    q_spec = lambda col0, j: pl.BlockSpec(
        (1, TM, HD_B), lambda bb, h, i: (bb, nsub * i + q_blk0 + j, col0 + h))
    kv_spec = lambda col0: pl.BlockSpec((1, n_keys, HD_B), lambda bb, h, i: (bb, 0, col0 + h))
    return pl.pallas_call(
        functools.partial(_attn_kernel, lam_init, nsub),
        grid=(b, nh, n_q // (nsub * TM)),
        in_specs=([q_spec(0, j) for j in range(nsub)] + [kv_spec(nh), kv_spec(2 * nh)]
                  + [q_spec(3 * nh, j) for j in range(nsub)]
                  + [pl.BlockSpec((4, HD_QK), lambda bb, h, i: (0, 0)),
                     pl.BlockSpec((1, HD_B), lambda bb, h, i: (0, 0))]),
        out_specs=pl.BlockSpec((1, nsub * TM, HD_B), lambda bb, h, i: (bb, i, h)),
        out_shape=jax.ShapeDtypeStruct((b, n_q, W_GRP), BF16),
        scratch_shapes=[pltpu.VMEM((n_keys, 2 * HD_B), BF16)],
        compiler_params=_params(("parallel", "parallel", "arbitrary")),
        name="diff_attention",
    )(*([pb] * (2 * nsub + 2)), lam_params, subln_g.reshape(1, HD_B))


def _seg_sum(x, ones_bd):
    hi, lo = _split_bf16(x)
    return _mm(hi, ones_bd) + _mm(lo, ones_bd)


def _head_diag(x4, mask):
    return jnp.where(mask, x4, 0.0).astype(BF16)


def _rwkv_prep_kernel(n_blk, rkv_ref, lo_ref, hrkv_ref, hlo_ref, mu_ref, mul_ref, w0_ref, w2_ref,
                      a0_ref, a2_ref, kk_ref, ka_ref, rk_ref,
                      g_ref, q_ref, rp_ref, yp_ref, bonus_ref):
    dr = pl.program_id(1)
    i = pl.program_id(2)
    fwd = dr == 0
    nl = RW_G // HD_C
    sgn = 1 - 2 * dr
    hrow = jnp.where(fwd, 15, 0)
    edge = jnp.logical_or(i == 0, i == jnp.where(fwd, 1, n_blk - 1))
    rsel = lax.broadcasted_iota(jnp.int32, (TM, 1), 0) == jnp.where(fwd, 0, TM - 1)
    hsel = lax.broadcasted_iota(jnp.int32, (16, 1), 0) == hrow

    def shifted(cur_ref, halo_ref, mu):
        f = cur_ref[0].astype(F32)
        halo = jnp.sum(jnp.where(hsel, halo_ref[0].astype(F32), 0.0), axis=0, keepdims=True)
        halo = jnp.where(edge, 0.0, halo)
        rolled = jnp.where(fwd, pltpu.roll(f, 1, axis=0), pltpu.roll(f, TM - 1, axis=0))
        prev = jnp.where(rsel, halo, rolled)
        return f + (prev - f) * mu

    z = shifted(rkv_ref, hrkv_ref, mu_ref[0])
    zl = shifted(lo_ref, hlo_ref, mul_ref[0])
    r, k, v = z[:, :W_GRP], z[:, W_GRP:2 * W_GRP], z[:, 2 * W_GRP:]
    wl = w0_ref[0] + _mm(jnp.tanh(zl).astype(BF16), w2_ref[0].astype(BF16))
    logd = -math.exp(-0.5) * _sigmoid(wl)
    a = _sigmoid(a0_ref[0] + _mm(zl.astype(BF16), a2_ref[0].astype(BF16)))
    ri = lax.broadcasted_iota(jnp.int32, (W_GRP, W_GRP), 0) // HD_C
    ci = lax.broadcasted_iota(jnp.int32, (W_GRP, W_GRP), 1) // HD_C
    ones_bd = jnp.where(ri == ci, 1.0, 0.0).astype(BF16)
    kk = k * kk_ref[0]
    kk = kk / jnp.maximum(jnp.sqrt(_seg_sum(kk * kk, ones_bd)), 1e-12)
    k2 = k * (1.0 + (a - 1.0) * ka_ref[0])
    bv = kk * a
    bonus_ref[0, 0] = _seg_sum(r * k2 * rk_ref[0], ones_bd) * v
    rt = lax.broadcasted_iota(jnp.int32, (TM, TM), 0)
    ct = lax.broadcasted_iota(jnp.int32, (TM, TM), 1)
    same = rt // RW_L == ct // RW_L
    tri = jnp.where(jnp.logical_and(same, (rt - ct) * sgn >= 0), 1.0, 0.0).astype(BF16)
    blk = jnp.where(same, 1.0, 0.0).astype(BF16)
    lh, ll = _split_bf16(logd)
    cum = _mm(tri, lh) + _mm(tri, ll)
    tot = _mm(blk, lh) + _mm(blk, ll)
    half = 0.5 * tot
    e_pos = jnp.exp(cum - half)
    e_neg = jnp.exp(half - cum)
    e_prev = jnp.exp(cum - logd - half)
    e_end = jnp.exp(tot - cum)
    g_end = jnp.exp(tot)
    e_half = jnp.exp(half)
    kt_all = kk * e_prev
    rt_all = r * e_pos
    kh_all = k2 * e_neg
    bh_all = bv * e_neg
    kb_all = k2 * e_end
    bb_all = bv * e_end
    row = lax.broadcasted_iota(jnp.int32, (RW_L, RW_G), 0)
    col = lax.broadcasted_iota(jnp.int32, (RW_L, RW_G), 1) % HD_C
    eye = jnp.where(col == row, 1.0, 0.0)
    r4 = lax.broadcasted_iota(jnp.int32, (RW_G, RW_G), 0) // HD_C
    c4 = lax.broadcasted_iota(jnp.int32, (RW_G, RW_G), 1) // HD_C
    dmask = r4 == c4
    r2 = lax.broadcasted_iota(jnp.int32, (2 * RW_L, RW_G), 0) % RW_L
    c2 = lax.broadcasted_iota(jnp.int32, (2 * RW_L, RW_G), 1) % HD_C
    first = jnp.where(lax.broadcasted_iota(jnp.int32, (2 * RW_L, RW_G), 0) < RW_L, 1, 0)
    mask2 = (r2 - c2) * sgn >= first

    def bd(x):
        return _head_diag(jnp.concatenate([x] * nl, axis=0), dmask)

    def hprod(x, y):
        return _mm(x.astype(BF16), bd(y))

    def diag_blocks(full):
        out = jnp.zeros((RW_L, RW_G), F32)
        lane_h = lax.broadcasted_iota(jnp.int32, (RW_L, RW_G), 1) // HD_C
        for hh in range(nl):
            out = out + jnp.where(lane_h == hh, full[hh * HD_C:(hh + 1) * HD_C, :], 0.0)
        return out

    pairs = [(slice(c * RW_L, (c + 1) * RW_L), slice(gi * RW_G, (gi + 1) * RW_G))
             for c in range(TM // RW_L) for gi in range(W_GRP // RW_G)]
    cat = lambda *xs: jnp.concatenate(xs, axis=0)
    L = RW_L
    kt = [kt_all[p] for p in pairs]
    rtl = [rt_all[p] for p in pairs]
    vv = [v[p] for p in pairs]
    lhs = [cat(a_, b_).astype(BF16) for a_, b_ in zip(kt, rtl)]
    pb_ = [jnp.where(mask2, _mm_nt(x, bd(bh_all[p])), 0.0) for x, p in zip(lhs, pairs)]
    pk_ = [jnp.where(mask2, _mm_nt(x, bd(kh_all[p])), 0.0) for x, p in zip(lhs, pairs)]
    res = [hprod(x, x[:L]) for x in pb_]
    tinv = [eye - x[:L] for x in pb_]
    mt = [x[L:] - y[L:] for x, y in zip(pb_, res)]
    pw = [y[:L] for y in res]
    for _ in range(4):
        res = [hprod(cat(t_, m_, w_), w_) for t_, m_, w_ in zip(tinv, mt, pw)]
        tinv = [t_ + y[:L] for t_, y in zip(tinv, res)]
        mt = [m_ + y[L:2 * L] for m_, y in zip(mt, res)]
        pw = [y[2 * L:] for y in res]
    res = [hprod(cat(t_, m_), w_) for t_, m_, w_ in zip(tinv, mt, pw)]
    tm = [cat(t_, m_) + y for t_, m_, y in zip(tinv, mt, res)]
    wv = [hprod(x, y) for x, y in zip(pk_, vv)]
    tmb = [x.astype(BF16) for x in tm]
    ww = [_mm(x, bd(y)) for x, y in zip(tmb, kt)]
    uu = [_mm(x, bd(y[:L])) for x, y in zip(tmb, wv)]
    for j, p in enumerate(pairs):
        rows, cols = p
        eh = e_half[rows.start:rows.start + 1, cols]
        rp_ref[0, 0, rows, cols] = (rtl[j] - ww[j][L:]) * eh
        yp_ref[0, 0, rows, cols] = wv[j][L:] - uu[j][L:]
        xt = cat(kb_all[p], -bb_all[p]).T.astype(BF16)
        zeros = jnp.zeros((L, RW_G), F32)
        yq = cat(vv[j], uu[j][:L]).astype(BF16)
        yg = cat(zeros, ww[j][:L]).astype(BF16)
        q_ref[0, 0, rows, cols] = diag_blocks(_mm(xt, yq))
        g_ref[0, 0, rows, cols] = (diag_blocks(_mm(xt, yg)) * eh
                                   + eye * g_end[rows.start:rows.start + 1, cols])


def _rwkv_scan_kernel(n_chain, *refs):
    ins, y_refs, h_ref = refs[:4 * n_chain], refs[4 * n_chain:5 * n_chain], refs[-1]
    nl = RW_G // HD_C

    @pl.when(pl.program_id(0) == 0)
    def _():
        h_ref[...] = jnp.zeros_like(h_ref)

    r4 = lax.broadcasted_iota(jnp.int32, (RW_G, RW_G), 0) // HD_C
    c4 = lax.broadcasted_iota(jnp.int32, (RW_G, RW_G), 1) // HD_C
    dmask = r4 == c4
    n_chunk = TM // RW_L
    links = [(j, slice(gi * RW_G, (gi + 1) * RW_G)) for j in range(n_chain)
             for gi in range(W_GRP // RW_G)]
    h = [h_ref[j, :, cols] for j, cols in links]
    for s in range(n_chunk):
        prods = []
        for (j, cols), hj in zip(links, h):
            g_ref, _, rp_ref, _ = ins[4 * j:4 * j + 4]
            c = n_chunk - 1 - s if j % 2 else s
            rows = slice(c * RW_L, (c + 1) * RW_L)
            h4 = jnp.where(dmask, jnp.concatenate([hj] * nl, axis=0), 0.0)
            hh, hl = _split_bf16(h4)
            lhs = jnp.concatenate([g_ref[0, 0, rows, cols], rp_ref[0, 0, rows, cols]], axis=0)
            lh, ll = _split_bf16(lhs)
            prods.append(_mm(lh, hh) + _mm(lh, hl) + _mm(ll, hh))
        for k, ((j, cols), prod) in enumerate(zip(links, prods)):
            _, q_ref, _, yp_ref = ins[4 * j:4 * j + 4]
            c = n_chunk - 1 - s if j % 2 else s
            rows = slice(c * RW_L, (c + 1) * RW_L)
            y_refs[j][0, 0, rows, cols] = prod[RW_L:] + yp_ref[0, 0, rows, cols]
            h[k] = prod[:RW_L] + q_ref[0, 0, rows, cols]
    for (j, cols), hj in zip(links, h):
        h_ref[j, :, cols] = hj


def _rwkv_merge_kernel(n_batch, *refs):
    y_refs = refs[:2 * n_batch]
    bonus_ref, g_ref, lw_ref, lb_ref, o_ref = refs[2 * n_batch:]
    ri = lax.broadcasted_iota(jnp.int32, (W_GRP, W_GRP), 0) // HD_C
    ci = lax.broadcasted_iota(jnp.int32, (W_GRP, W_GRP), 1) // HD_C
    ones_bd = jnp.where(ri == ci, 1.0, 0.0).astype(BF16)
    for bb in range(n_batch):
        y = y_refs[2 * bb][0, 0] + y_refs[2 * bb + 1][0, 0]
        mean = _seg_sum(y, ones_bd) * (1.0 / HD_C)
        yc = y - mean
        var = _seg_sum(yc * yc, ones_bd) * (1.0 / HD_C)
        yn = yc * lax.rsqrt(var + GN_EPS) * lw_ref[...] + lb_ref[...]
        out = yn + bonus_ref[bb, 0] + bonus_ref[bb, 1]
        o_ref[bb] = (out * _silu(g_ref[bb].astype(F32))).astype(BF16)


def _scan_block(dr, i, n_blk):
    return jnp.where(dr == 0, i, jnp.where(i == 0, 0, n_blk - i))


def _rwkv(pc, p):
    b, t, _ = pc.shape
    nblk = t // TM
    hb = TM // 16
    lblk = 4 * W_GRP // LORA_PAD

    def halo_idx(dr, i):
        return jnp.where(dr == 0, jnp.maximum(i * hb - 1, 0), jnp.minimum((i + 1) * hb, nblk * hb - 1))

    per_dir = lambda n: pl.BlockSpec((1, 1, n), lambda bb, dr, i: (dr, 0, 0))
    per_dir_m = lambda m, n: pl.BlockSpec((1, m, n), lambda bb, dr, i: (dr, 0, 0))
    state = lambda: pl.BlockSpec((1, 1, TM, W_GRP), lambda bb, dr, i: (bb, dr, i, 0))
    sds = jax.ShapeDtypeStruct((b, 2, t, W_GRP), F32)
    gm, qm, rp, yp, bonus = pl.pallas_call(
        functools.partial(_rwkv_prep_kernel, nblk),
        grid=(b, 2, nblk),
        in_specs=[pl.BlockSpec((1, TM, 3 * W_GRP), lambda bb, dr, i: (bb, i, 0)),
                  pl.BlockSpec((1, TM, LORA_PAD), lambda bb, dr, i: (bb, i, lblk + dr)),
                  pl.BlockSpec((1, 16, 3 * W_GRP), lambda bb, dr, i: (bb, halo_idx(dr, i), 0)),
                  pl.BlockSpec((1, 16, LORA_PAD), lambda bb, dr, i: (bb, halo_idx(dr, i), lblk + dr)),
                  per_dir(3 * W_GRP), per_dir(LORA_PAD), per_dir(W_GRP), per_dir_m(LORA_PAD, W_GRP),
                  per_dir(W_GRP), per_dir_m(LORA_PAD, W_GRP), per_dir(W_GRP), per_dir(W_GRP),
                  per_dir(W_GRP)],
        out_specs=[state()] * 5,
        out_shape=[sds] * 5,
        compiler_params=_params(("parallel", "parallel", "parallel")),
        name="rwkv_prep",
    )(pc, pc, pc, pc, p["mu_rkv"], p["mu_lora"], p["w0"], p["w2"], p["a0"], p["a2"],
      p["kk"], p["ka"], p["rk"])

    chains = [(bb, dr) for bb in range(b) for dr in range(2)]
    scan_spec = lambda bb, dr: pl.BlockSpec(
        (1, 1, TM, W_GRP), lambda i: (bb, dr, _scan_block(dr, i, nblk), 0))
    sds1 = jax.ShapeDtypeStruct((1, 1, t, W_GRP), F32)
    ys = pl.pallas_call(
        functools.partial(_rwkv_scan_kernel, len(chains)),
        grid=(nblk,),
        in_specs=[scan_spec(bb, dr) for bb, dr in chains for _ in range(4)],
        out_specs=[pl.BlockSpec((1, 1, TM, W_GRP), lambda i, dr=dr: (0, 0, _scan_block(dr, i, nblk), 0))
                   for _, dr in chains],
        out_shape=[sds1] * len(chains),
        scratch_shapes=[pltpu.VMEM((len(chains), RW_L, W_GRP), F32)],
        compiler_params=_params(("arbitrary",)),
        name="rwkv_scan",
    )(*[a for _ in chains for a in (gm, qm, rp, yp)])

    return pl.pallas_call(
        functools.partial(_rwkv_merge_kernel, b),
        grid=(nblk,),
        in_specs=[pl.BlockSpec((1, 1, TM, W_GRP), lambda i: (0, 0, i, 0))] * len(chains) + [
                  pl.BlockSpec((b, 2, TM, W_GRP), lambda i: (0, 0, i, 0)),
                  pl.BlockSpec((b, TM, W_GRP), lambda i: (0, i, 3)),
                  pl.BlockSpec((1, W_GRP), lambda i: (0, 0)),
                  pl.BlockSpec((1, W_GRP), lambda i: (0, 0))],
        out_specs=pl.BlockSpec((b, TM, W_GRP), lambda i: (0, i, 0)),
        out_shape=jax.ShapeDtypeStruct((b, t, W_GRP), BF16),
        compiler_params=_params(("parallel",)),
        name="rwkv_merge",
    )(*ys, bonus, pc, p["ln_w"], p["ln_b"])


def _outproj_kernel(mod_row, oa_ref, ob_ref, oc_ref, od_ref, w_ref, x_ref, mod_ref, g_ref, o_ref):
    d = x_ref.shape[-1]
    b = pl.program_id(0)
    acc = _mm(oa_ref[0], w_ref[0, 0:W_GRP, :])
    acc = acc + _mm(ob_ref[0], w_ref[0, W_GRP:2 * W_GRP, :])
    acc = acc + _mm(oc_ref[0], w_ref[0, 2 * W_GRP:3 * W_GRP, :])
    acc = acc + _mm(od_ref[0], w_ref[0, 3 * W_GRP:, :])
    y = acc * lax.rsqrt(jnp.mean(acc * acc, axis=-1, keepdims=True) + EPS) * g_ref[...]
    row = mod_ref[0, pl.ds(b if mod_row is None else mod_row, 1), :]
    o_ref[0] = x_ref[0] + row[:, 2 * d:] * y


def _outproj(mix, w_out, x, mods, layer, g_post, blk0, mod_row):
    b, n, d = x.shape
    o_spec = lambda off: pl.BlockSpec((1, TM, W_GRP), lambda bb, i: (bb, i + off, 0))
    return pl.pallas_call(
        functools.partial(_outproj_kernel, mod_row),
        grid=(b, n // TM),
        in_specs=[o_spec(off) for off in blk0] + [
            pl.BlockSpec((1,) + w_out.shape[1:], lambda bb, i: (layer, 0, 0)),
            pl.BlockSpec((1, TM, d), lambda bb, i: (bb, i, 0)),
            pl.BlockSpec((1, 8, 3 * d), lambda bb, i: (layer, 0, 0)),
            pl.BlockSpec((1, d), lambda bb, i: (0, 0))],
        out_specs=pl.BlockSpec((1, TM, d), lambda bb, i: (bb, i, 0)),
        out_shape=jax.ShapeDtypeStruct((b, n, d), F32),
        compiler_params=_params(("parallel", "parallel")),
        name="outproj",
    )(*mix, w_out, x, mods, g_post.reshape(1, d))


def _w_in_segments():
    g = W_GRP
    o = [0]
    for s in (g, g, g, g, g, g, g, 3 * g, LORA_W + LORA_A, LORA_W + LORA_A, g, g, g, g, g):
        o.append(o[-1] + s)
    order = [0, 1, 2, 3, 4, 5, 6, 12, 13, 11, 14, 7, 10, 8, 9]
    segs, dst = [], 0
    for j in order:
        width = o[j + 1] - o[j]
        segs.append((o[j], width, dst))
        dst += -(-width // LORA_PAD) * LORA_PAD
    assert dst == N_ALL
    return segs, o[-1]


def _w_in_kernel(w_ref, o_ref):
    segs, _ = _w_in_segments()
    rows = w_ref.shape[1]
    for src, width, dst in segs:
        piece = w_ref[0, :, src:src + width].astype(BF16)
        padded = -(-width // LORA_PAD) * LORA_PAD
        if padded != width:
            piece = jnp.concatenate([piece, jnp.zeros((rows, padded - width), BF16)], axis=1)
        o_ref[0, :, dst:dst + padded] = piece


def _regroup_w_in(w_in):
    depth, d, n_in = w_in.shape
    assert n_in == _w_in_segments()[1]
    tr = 256
    return pl.pallas_call(
        _w_in_kernel,
        grid=(depth, d // tr),
        in_specs=[pl.BlockSpec((1, tr, n_in), lambda l, i: (l, i, 0))],
        out_specs=pl.BlockSpec((1, tr, N_ALL), lambda l, i: (l, i, 0)),
        out_shape=jax.ShapeDtypeStruct((depth, d, N_ALL), BF16),
        compiler_params=_params(("parallel", "parallel")),
        name="w_in_prep",
    )(w_in)


def _cast_kernel(w_ref, o_ref):
    o_ref[...] = w_ref[...].astype(BF16)


def _cast_w_out(w_out):
    depth, k, d = w_out.shape
    tr = 512
    return pl.pallas_call(
        _cast_kernel,
        grid=(depth, k // tr),
        in_specs=[pl.BlockSpec((1, tr, d), lambda l, i: (l, i, 0))],
        out_specs=pl.BlockSpec((1, tr, d), lambda l, i: (l, i, 0)),
        out_shape=jax.ShapeDtypeStruct((depth, k, d), BF16),
        compiler_params=_params(("parallel", "parallel")),
        name="w_out_prep",
    )(w_out)


def _rope_tables(n_ctx, n_lat):
    n_rows = n_lat // GRID_W
    inv = ROPE_BASE ** (-jnp.arange(0, 32, 2, dtype=F32) / 32)
    lane = jnp.arange(HD_B)
    use_col = ((lane // 32) % 2 == 1)[None, None, :]
    trig = lambda f, n: f(jnp.arange(n, dtype=F32)[:, None] * inv[lane % 16][None, :])
    grid = lambda f: jnp.where(use_col, trig(f, GRID_W)[None, :, :], trig(f, n_rows)[:, None, :]
                               ).reshape(n_lat, HD_B)
    first = (lane % 32 < 16)[None, :]
    cos, sin = grid(jnp.cos), grid(jnp.sin)
    sa = jnp.where(first, -sin, 0.0)
    sb = jnp.where(first, 0.0, sin)
    ident = lambda v: jnp.full((n_ctx, HD_B), v, F32)
    return (jnp.concatenate([ident(1.0), cos]), jnp.concatenate([ident(0.0), sa]),
            jnp.concatenate([ident(0.0), sb]))


def _rwkv_params(l, mu, w0, w2, a0, a2, kk, ka, rk, ln_w, ln_b):
    g = W_GRP
    zpad = lambda n: jnp.zeros((2, n), F32)
    mu_l = jnp.concatenate([mu[l][:, 3 * g:], zpad(LORA_PAD - LORA_W - LORA_A)], axis=1)
    w2p = jnp.concatenate([w2[l], jnp.zeros((2, LORA_PAD - LORA_W, g), F32)], axis=1)
    a2p = jnp.concatenate([jnp.zeros((2, LORA_W, g), F32), a2[l],
                           jnp.zeros((2, LORA_PAD - LORA_W - LORA_A, g), F32)], axis=1)
    r3 = lambda v: v.reshape(2, 1, -1)
    return dict(mu_rkv=r3(mu[l][:, :3 * g]), mu_lora=r3(mu_l), w0=r3(w0[l]), w2=w2p, a0=r3(a0[l]),
                a2=a2p, kk=r3(kk[l]), ka=r3(ka[l]), rk=r3(rk[l].reshape(2, g)),
                ln_w=ln_w[l].reshape(1, g), ln_b=ln_b[l].reshape(1, g))


def kernel(x, c, ctx, c_ctx, w_mod, b_mod, g_pre, g_post, w_in, w_out, sgu_w, sgu_b, lam_q1, lam_k1, lam_q2, lam_k2, subln_g, rwkv_mu, rwkv_w0, rwkv_w2, rwkv_a0, rwkv_a2, rwkv_kk, rwkv_ka, rwkv_rk, rwkv_ln_w, rwkv_ln_b, conv_w):
    depth = w_mod.shape[0]
    b, n_lat, _ = x.shape
    n_ctx = ctx.shape[1]
    mods = _modulation(c, c_ctx, w_mod, b_mod)
    rope = _rope_tables(n_ctx, n_lat)
    w_in_b = _regroup_w_in(w_in)
    w_out_b = _cast_w_out(w_out)
    xc = ctx
    for l in range(depth):
        need_ctx = l < depth - 1
        lam_init = 0.8 - 0.6 * math.exp(-0.3 * l)
        pa, pb, pd, pc = _inproj(xc, x, mods, l, g_pre[l], w_in_b, rope)
        o_a, o_d = _local_mixers(pa, pd, sgu_w[l], sgu_b[l], conv_w[l])
        lam_params = jnp.stack([lam_q1[l], lam_k1[l], lam_q2[l], lam_k2[l]])
        cb = n_ctx // TM
        o_b = _attention(pb, lam_params, subln_g[l], lam_init, cb, n_lat, n_ctx + n_lat, 2)
        o_c = _rwkv(pc, _rwkv_params(l, rwkv_mu, rwkv_w0, rwkv_w2, rwkv_a0, rwkv_a2, rwkv_kk,
                                     rwkv_ka, rwkv_rk, rwkv_ln_w, rwkv_ln_b))
        x_new = _outproj((o_a, o_b, o_c, o_d), w_out_b, x, mods, l, g_post[l], (cb, 0, cb, cb), None)
        if need_ctx:
            o_bc = _attention(pb, lam_params, subln_g[l], lam_init, 0, n_ctx, n_ctx, 1)
            xc = _outproj((o_a, o_bc, o_c, o_d), w_out_b, xc, mods, l, g_post[l], (0, 0, 0, 0), b)
        x = x_new
    return x
```

```python
import functools
import math

import jax
import jax.numpy as jnp
from jax import lax
from jax.experimental import pallas as pl
from jax.experimental.pallas import tpu as pltpu

F32 = jnp.float32
BF16 = jnp.bfloat16

EPS = 1e-6
GN_EPS = 64e-5
GRID_W = 64
ROPE_BASE = 10000.0
LOG2E = 1.4426950408889634

W_GRP = 512
CHUNK_A = 128
HD_B = 128
HD_QK = 64
HD_C = 64
LORA_W = 64
LORA_A = 32
LORA_PAD = 128
RW_L = 64
RW_G = 256
TM = 256
ATT_KB = 512
ATT_NSUB = 4
ATT_CHAIN = 2
VMEM_LIMIT = 56 * 1024 * 1024

N_A = 3 * W_GRP
N_B = 4 * W_GRP
N_D = 4 * W_GRP
N_C = 4 * W_GRP + 2 * LORA_PAD
OFF_A, OFF_B, OFF_D, OFF_C = 0, N_A, N_A + N_B, N_A + N_B + N_D
N_ALL = OFF_C + N_C


def _silu(g):
    return g * (1.0 / (1.0 + jnp.exp(-g)))


def _sigmoid(x):
    return 1.0 / (1.0 + jnp.exp(-x))


def _split_bf16(x):
    hi = x.astype(BF16)
    lo = (x - hi.astype(F32)).astype(BF16)
    return hi, lo


def _mm(a, b):
    return jnp.dot(a, b, preferred_element_type=F32)


def _mm_nt(a, b):
    return lax.dot_general(a, b, (((1,), (1,)), ((), ())), preferred_element_type=F32)


def _params(sem):
    return pltpu.CompilerParams(dimension_semantics=sem, vmem_limit_bytes=VMEM_LIMIT)


def _mod_kernel(s_ref, w_ref, b_ref, o_ref):
    s = _silu(s_ref[...])
    o_ref[0] = _mm(s.astype(BF16), w_ref[0].astype(BF16)) + b_ref[0]


def _modulation(c, c_ctx, w_mod, b_mod):
    depth, d, n3 = w_mod.shape
    b = c.shape[0]
    rows = jnp.zeros((8, d), F32).at[:b].set(c).at[b].set(c_ctx)
    tn = 768
    return pl.pallas_call(
        _mod_kernel,
        grid=(depth, n3 // tn),
        in_specs=[pl.BlockSpec((8, d), lambda l, j: (0, 0)),
                  pl.BlockSpec((1, d, tn), lambda l, j: (l, 0, j)),
                  pl.BlockSpec((1, 1, tn), lambda l, j: (l, 0, j))],
        out_specs=pl.BlockSpec((1, 8, tn), lambda l, j: (l, 0, j)),
        out_shape=jax.ShapeDtypeStruct((depth, 8, n3), F32),
        compiler_params=_params(("parallel", "parallel")),
        name="modulation",
    )(rows, w_mod, b_mod.reshape(depth, 1, n3))


def _inproj_kernel(n_batch, xc_ref, xl_ref, mod_ref, g_ref, w_ref, cos_ref, sa_ref, sb_ref,
                   pa_ref, pb_ref, pd_ref, pc_ref):
    b = pl.program_id(0)
    i = pl.program_id(1)
    d = xc_ref.shape[-1]
    is_ctx = i == 0
    x = jnp.where(is_ctx, xc_ref[0], xl_ref[0])
    y = x * lax.rsqrt(jnp.mean(x * x, axis=-1, keepdims=True) + EPS) * g_ref[...]
    row = jnp.where(is_ctx, mod_ref[0, pl.ds(n_batch, 1), :], mod_ref[0, pl.ds(b, 1), :])
    h = (y * (1.0 + row[:, d:2 * d]) + row[:, :d]).astype(BF16)

    def proj(c0, c1):
        return _mm(h, w_ref[0, :, c0:c1])

    pa_ref[0] = proj(OFF_A, OFF_A + N_A).astype(BF16)
    cos, sa, sb = cos_ref[...], sa_ref[...], sb_ref[...]
    qk = proj(OFF_B, OFF_B + 2 * W_GRP)
    for j in range(2 * W_GRP // HD_B):
        t = qk[:, j * HD_B:(j + 1) * HD_B]
        t = t * cos + pltpu.roll(t, HD_B - 16, axis=1) * sa + pltpu.roll(t, 16, axis=1) * sb
        if j < W_GRP // HD_B:
            t = t * (HD_QK ** -0.5 * LOG2E)
        pb_ref[0, :, j * HD_B:(j + 1) * HD_B] = t.astype(BF16)
    pb_ref[0, :, 2 * W_GRP:] = proj(OFF_B + 2 * W_GRP, OFF_B + N_B).astype(BF16)
    pd_ref[0] = proj(OFF_D, OFF_D + N_D).astype(BF16)
    pc_ref[0] = proj(OFF_C, OFF_C + N_C).astype(BF16)


def _inproj(xc, xl, mods, layer, g_pre, w_in, rope):
    b, n_ctx, d = xc.shape
    n_lat = xl.shape[1]
    t = n_ctx + n_lat
    assert n_ctx == TM and n_lat % TM == 0
    nblk = t // TM
    cos, sa, sb = rope
    tab = pl.BlockSpec((TM, HD_B), lambda bb, i: (i, 0))
    out = lambda n: pl.BlockSpec((1, TM, n), lambda bb, i: (bb, i, 0))
    return pl.pallas_call(
        functools.partial(_inproj_kernel, b),
        grid=(b, nblk),
        in_specs=[pl.BlockSpec((1, TM, d), lambda bb, i: (bb, 0, 0)),
                  pl.BlockSpec((1, TM, d), lambda bb, i: (bb, jnp.maximum(i - 1, 0), 0)),
                  pl.BlockSpec((1, 8, 3 * d), lambda bb, i: (layer, 0, 0)),
                  pl.BlockSpec((1, d), lambda bb, i: (0, 0)),
                  pl.BlockSpec((1, d, N_ALL), lambda bb, i: (layer, 0, 0), pipeline_mode=pl.Buffered(1)),
                  tab, tab, tab],
        out_specs=[out(N_A), out(N_B), out(N_D), out(N_C)],
        out_shape=[jax.ShapeDtypeStruct((b, t, n), BF16) for n in (N_A, N_B, N_D, N_C)],
        compiler_params=_params(("parallel", "arbitrary")),
        name="inproj",
    )(xc, xl, mods, g_pre.reshape(1, d), w_in, cos, sa, sb)


def _local_kernel(n_blk, pa_ref, pd_ref, hp_ref, hn_ref, ws_ref, bs_ref, cw_ref, oa_ref, od_ref):
    i = pl.program_id(1)
    for c in range(TM // CHUNK_A):
        rows = slice(c * CHUNK_A, (c + 1) * CHUNK_A)
        for h in range(W_GRP // CHUNK_A):
            cols = slice(h * CHUNK_A, (h + 1) * CHUNK_A)
            u = pa_ref[0, rows, cols].astype(F32)
            v = pa_ref[0, rows, W_GRP + h * CHUNK_A:W_GRP + (h + 1) * CHUNK_A].astype(F32)
            g = pa_ref[0, rows, 2 * W_GRP + h * CHUNK_A:2 * W_GRP + (h + 1) * CHUNK_A].astype(F32)
            mean = jnp.mean(v, axis=-1, keepdims=True)
            vc = v - mean
            vn = vc * lax.rsqrt(jnp.mean(vc * vc, axis=-1, keepdims=True) + EPS)
            mixed = _mm(ws_ref[h].astype(BF16), vn.astype(BF16)) + bs_ref[:, h:h + 1]
            oa_ref[0, rows, cols] = (u * mixed * _silu(g)).astype(BF16)
    cx = pd_ref[0, :, 0:2 * W_GRP].astype(F32)
    z = cx[:, :W_GRP] * cx[:, W_GRP:]
    hp = hp_ref[0].astype(F32)
    hn = hn_ref[0].astype(F32)
    zp = hp[15:16, :W_GRP] * hp[15:16, W_GRP:]
    zn = hn[0:1, :W_GRP] * hn[0:1, W_GRP:]
    zp = jnp.where(i <= 1, 0.0, zp)
    zn = jnp.where(jnp.logical_or(i == 0, i == n_blk - 1), 0.0, zn)
    r = lax.broadcasted_iota(jnp.int32, (TM, 1), 0)
    z_prev = jnp.where(r == 0, zp, pltpu.roll(z, 1, axis=0))
    z_next = jnp.where(r == TM - 1, zn, pltpu.roll(z, TM - 1, axis=0))
    w = cw_ref[...]
    y = z_prev * w[0:1] + z * w[1:2] + z_next * w[2:3]
    bg = pd_ref[0, :, 2 * W_GRP:3 * W_GRP].astype(F32)
    g = pd_ref[0, :, 3 * W_GRP:].astype(F32)
    od_ref[0] = (bg * y * _silu(g)).astype(BF16)


def _local_mixers(pa, pd, sgu_w, sgu_b, conv_w):
    b, t, _ = pa.shape
    nblk = t // TM
    hb = TM // 16
    return pl.pallas_call(
        functools.partial(_local_kernel, nblk),
        grid=(b, nblk),
        in_specs=[pl.BlockSpec((1, TM, N_A), lambda bb, i: (bb, i, 0)),
                  pl.BlockSpec((1, TM, N_D), lambda bb, i: (bb, i, 0)),
                  pl.BlockSpec((1, 16, 2 * W_GRP), lambda bb, i: (bb, jnp.maximum(i * hb - 1, 0), 0)),
                  pl.BlockSpec((1, 16, 2 * W_GRP),
                               lambda bb, i: (bb, jnp.minimum((i + 1) * hb, nblk * hb - 1), 0)),
                  pl.BlockSpec(sgu_w.shape, lambda bb, i: (0, 0, 0)),
                  pl.BlockSpec((CHUNK_A, sgu_w.shape[0]), lambda bb, i: (0, 0)),
                  pl.BlockSpec(conv_w.shape, lambda bb, i: (0, 0))],
        out_specs=[pl.BlockSpec((1, TM, W_GRP), lambda bb, i: (bb, i, 0))] * 2,
        out_shape=[jax.ShapeDtypeStruct((b, t, W_GRP), BF16)] * 2,
        compiler_params=_params(("parallel", "parallel")),
        name="local_mixers",
    )(pa, pd, pd, pd, sgu_w, sgu_b.T, conv_w)


def _attn_kernel(lam_init, nsub, *refs):
    q_refs, (k_ref, v_ref), g_refs = refs[:nsub], refs[nsub:nsub + 2], refs[nsub + 2:2 * nsub + 2]
    lam_ref, sg_ref, o_ref, vext_ref = refs[2 * nsub + 2:]
    tq = nsub * TM
    n_keys = k_ref.shape[1]

    @pl.when(pl.program_id(2) == 0)
    def _():
        vext_ref[:, :HD_B] = v_ref[0]
        vext_ref[:, HD_B:] = jnp.ones((n_keys, HD_B), BF16)

    lp = lam_ref[...]
    lam = (jnp.exp(jnp.sum(lp[0:1] * lp[1:2], axis=-1, keepdims=True))
           - jnp.exp(jnp.sum(lp[2:3] * lp[3:4], axis=-1, keepdims=True)) + lam_init)
    lane = lax.broadcasted_iota(jnp.int32, (1, HD_B), 1)
    n_chain = -(-nsub // ATT_CHAIN)
    tc = tq // n_chain
    q2s = []
    for ci in range(n_chain):
        q = jnp.concatenate([r[0] for r in q_refs[ci * ATT_CHAIN:(ci + 1) * ATT_CHAIN]], axis=0)
        zero = jnp.zeros_like(q)
        q2s.append(jnp.concatenate([jnp.where(lane < HD_QK, q, zero),
                                    jnp.where(lane >= HD_QK, q, zero)], axis=0))
    ms = [None] * n_chain
    os_ = [None] * n_chain
    for k0 in range(0, n_keys, ATT_KB):
        k1 = min(k0 + ATT_KB, n_keys)
        for ci in range(n_chain):
            s = _mm_nt(q2s[ci], k_ref[0, k0:k1])
            bm = jnp.max(s, axis=-1, keepdims=True)
            if ms[ci] is None:
                ms[ci] = bm
                os_[ci] = _mm(jnp.exp2(s - bm).astype(BF16), vext_ref[k0:k1])
            else:
                m_new = jnp.maximum(ms[ci], bm)
                os_[ci] = (os_[ci] * jnp.exp2(ms[ci] - m_new)
                           + _mm(jnp.exp2(s - m_new).astype(BF16), vext_ref[k0:k1]))
                ms[ci] = m_new
    for ci in range(n_chain):
        o = os_[ci]
        a = o[:tc, :HD_B] / o[:tc, HD_B:] - lam * (o[tc:, :HD_B] / o[tc:, HD_B:])
        y = a * lax.rsqrt(jnp.mean(a * a, axis=-1, keepdims=True) + EPS) * sg_ref[...] * (1.0 - lam_init)
        g = jnp.concatenate([r[0] for r in g_refs[ci * ATT_CHAIN:(ci + 1) * ATT_CHAIN]], axis=0)
        o_ref[0, ci * tc:(ci + 1) * tc, :] = (y * _silu(g.astype(F32))).astype(BF16)


def _attention(pb, lam_params, subln_g, lam_init, q_blk0, n_q, n_keys, nsub):
    b = pb.shape[0]
    nh = W_GRP // HD_B
    q_spec = lambda col0, j: pl.BlockSpec(
        (1, TM, HD_B), lambda bb, h, i: (bb, nsub * i + q_blk0 + j, col0 + h))
    kv_spec = lambda col0: pl.BlockSpec((1, n_keys, HD_B), lambda bb, h, i: (bb, 0, col0 + h))
    return pl.pallas_call(
        functools.partial(_attn_kernel, lam_init, nsub),
        grid=(b, nh, n_q // (nsub * TM)),
        in_specs=([q_spec(0, j) for j in range(nsub)] + [kv_spec(nh), kv_spec(2 * nh)]
                  + [q_spec(3 * nh, j) for j in range(nsub)]
                  + [pl.BlockSpec((4, HD_QK), lambda bb, h, i: (0, 0)),
                     pl.BlockSpec((1, HD_B), lambda bb, h, i: (0, 0))]),
        out_specs=pl.BlockSpec((1, nsub * TM, HD_B), lambda bb, h, i: (bb, i, h)),
        out_shape=jax.ShapeDtypeStruct((b, n_q, W_GRP), BF16),
        scratch_shapes=[pltpu.VMEM((n_keys, 2 * HD_B), BF16)],
        compiler_params=_params(("parallel", "parallel", "arbitrary")),
        name="diff_attention",
    )(*([pb] * (2 * nsub + 2)), lam_params, subln_g.reshape(1, HD_B))


def _seg_sum(x, ones_bd):
    hi, lo = _split_bf16(x)
    return _mm(hi, ones_bd) + _mm(lo, ones_bd)


def _head_diag(x4, mask):
    return jnp.where(mask, x4, 0.0).astype(BF16)


def _rwkv_prep_kernel(n_blk, rkv_ref, lo_ref, hrkv_ref, hlo_ref, mu_ref, mul_ref, w0_ref, w2_ref,
                      a0_ref, a2_ref, kk_ref, ka_ref, rk_ref,
                      g_ref, q_ref, rp_ref, yp_ref, bonus_ref):
    dr = pl.program_id(1)
    i = pl.program_id(2)
    fwd = dr == 0
    nl = RW_G // HD_C
    sgn = 1 - 2 * dr
    hrow = jnp.where(fwd, 15, 0)
    edge = jnp.logical_or(i == 0, i == jnp.where(fwd, 1, n_blk - 1))
    rsel = lax.broadcasted_iota(jnp.int32, (TM, 1), 0) == jnp.where(fwd, 0, TM - 1)
    hsel = lax.broadcasted_iota(jnp.int32, (16, 1), 0) == hrow

    def shifted(cur_ref, halo_ref, mu):
        f = cur_ref[0].astype(F32)
        halo = jnp.sum(jnp.where(hsel, halo_ref[0].astype(F32), 0.0), axis=0, keepdims=True)
        halo = jnp.where(edge, 0.0, halo)
        rolled = jnp.where(fwd, pltpu.roll(f, 1, axis=0), pltpu.roll(f, TM - 1, axis=0))
        prev = jnp.where(rsel, halo, rolled)
        return f + (prev - f) * mu

    z = shifted(rkv_ref, hrkv_ref, mu_ref[0])
    zl = shifted(lo_ref, hlo_ref, mul_ref[0])
    r, k, v = z[:, :W_GRP], z[:, W_GRP:2 * W_GRP], z[:, 2 * W_GRP:]
    wl = w0_ref[0] + _mm(jnp.tanh(zl).astype(BF16), w2_ref[0].astype(BF16))
    logd = -math.exp(-0.5) * _sigmoid(wl)
    a = _sigmoid(a0_ref[0] + _mm(zl.astype(BF16), a2_ref[0].astype(BF16)))
    ri = lax.broadcasted_iota(jnp.int32, (W_GRP, W_GRP), 0) // HD_C
    ci = lax.broadcasted_iota(jnp.int32, (W_GRP, W_GRP), 1) // HD_C
    ones_bd = jnp.where(ri == ci, 1.0, 0.0).astype(BF16)
    kk = k * kk_ref[0]
    kk = kk / jnp.maximum(jnp.sqrt(_seg_sum(kk * kk, ones_bd)), 1e-12)
    k2 = k * (1.0 + (a - 1.0) * ka_ref[0])
    bv = kk * a
    bonus_ref[0, 0] = _seg_sum(r * k2 * rk_ref[0], ones_bd) * v
    rt = lax.broadcasted_iota(jnp.int32, (TM, TM), 0)
    ct = lax.broadcasted_iota(jnp.int32, (TM, TM), 1)
    same = rt // RW_L == ct // RW_L
    tri = jnp.where(jnp.logical_and(same, (rt - ct) * sgn >= 0), 1.0, 0.0).astype(BF16)
    blk = jnp.where(same, 1.0, 0.0).astype(BF16)
    lh, ll = _split_bf16(logd)
    cum = _mm(tri, lh) + _mm(tri, ll)
    tot = _mm(blk, lh) + _mm(blk, ll)
    half = 0.5 * tot
    e_pos = jnp.exp(cum - half)
    e_neg = jnp.exp(half - cum)
    e_prev = jnp.exp(cum - logd - half)
    e_end = jnp.exp(tot - cum)
    g_end = jnp.exp(tot)
    e_half = jnp.exp(half)
    kt_all = kk * e_prev
    rt_all = r * e_pos
    kh_all = k2 * e_neg
    bh_all = bv * e_neg
    kb_all = k2 * e_end
    bb_all = bv * e_end
    row = lax.broadcasted_iota(jnp.int32, (RW_L, RW_G), 0)
    col = lax.broadcasted_iota(jnp.int32, (RW_L, RW_G), 1) % HD_C
    eye = jnp.where(col == row, 1.0, 0.0)
    r4 = lax.broadcasted_iota(jnp.int32, (RW_G, RW_G), 0) // HD_C
    c4 = lax.broadcasted_iota(jnp.int32, (RW_G, RW_G), 1) // HD_C
    dmask = r4 == c4
    r2 = lax.broadcasted_iota(jnp.int32, (2 * RW_L, RW_G), 0) % RW_L
    c2 = lax.broadcasted_iota(jnp.int32, (2 * RW_L, RW_G), 1) % HD_C
    first = jnp.where(lax.broadcasted_iota(jnp.int32, (2 * RW_L, RW_G), 0) < RW_L, 1, 0)
    mask2 = (r2 - c2) * sgn >= first

    def bd(x):
        return _head_diag(jnp.concatenate([x] * nl, axis=0), dmask)

    def hprod(x, y):
        return _mm(x.astype(BF16), bd(y))

    def diag_blocks(full):
        out = jnp.zeros((RW_L, RW_G), F32)
        lane_h = lax.broadcasted_iota(jnp.int32, (RW_L, RW_G), 1) // HD_C
        for hh in range(nl):
            out = out + jnp.where(lane_h == hh, full[hh * HD_C:(hh + 1) * HD_C, :], 0.0)
        return out

    pairs = [(slice(c * RW_L, (c + 1) * RW_L), slice(gi * RW_G, (gi + 1) * RW_G))
             for c in range(TM // RW_L) for gi in range(W_GRP // RW_G)]
    cat = lambda *xs: jnp.concatenate(xs, axis=0)
    L = RW_L
    kt = [kt_all[p] for p in pairs]
    rtl = [rt_all[p] for p in pairs]
    vv = [v[p] for p in pairs]
    lhs = [cat(a_, b_).astype(BF16) for a_, b_ in zip(kt, rtl)]
    pb_ = [jnp.where(mask2, _mm_nt(x, bd(bh_all[p])), 0.0) for x, p in zip(lhs, pairs)]
    pk_ = [jnp.where(mask2, _mm_nt(x, bd(kh_all[p])), 0.0) for x, p in zip(lhs, pairs)]
    res = [hprod(x, x[:L]) for x in pb_]
    tinv = [eye - x[:L] for x in pb_]
    mt = [x[L:] - y[L:] for x, y in zip(pb_, res)]
    pw = [y[:L] for y in res]
    for _ in range(4):
        res = [hprod(cat(t_, m_, w_), w_) for t_, m_, w_ in zip(tinv, mt, pw)]
        tinv = [t_ + y[:L] for t_, y in zip(tinv, res)]
        mt = [m_ + y[L:2 * L] for m_, y in zip(mt, res)]
        pw = [y[2 * L:] for y in res]
    res = [hprod(cat(t_, m_), w_) for t_, m_, w_ in zip(tinv, mt, pw)]
    tm = [cat(t_, m_) + y for t_, m_, y in zip(tinv, mt, res)]
    wv = [hprod(x, y) for x, y in zip(pk_, vv)]
    tmb = [x.astype(BF16) for x in tm]
    ww = [_mm(x, bd(y)) for x, y in zip(tmb, kt)]
    uu = [_mm(x, bd(y[:L])) for x, y in zip(tmb, wv)]
    for j, p in enumerate(pairs):
        rows, cols = p
        eh = e_half[rows.start:rows.start + 1, cols]
        rp_ref[0, 0, rows, cols] = (rtl[j] - ww[j][L:]) * eh
        yp_ref[0, 0, rows, cols] = wv[j][L:] - uu[j][L:]
        xt = cat(kb_all[p], -bb_all[p]).T.astype(BF16)
        zeros = jnp.zeros((L, RW_G), F32)
        yq = cat(vv[j], uu[j][:L]).astype(BF16)
        yg = cat(zeros, ww[j][:L]).astype(BF16)
        q_ref[0, 0, rows, cols] = diag_blocks(_mm(xt, yq))
        g_ref[0, 0, rows, cols] = (diag_blocks(_mm(xt, yg)) * eh
                                   + eye * g_end[rows.start:rows.start + 1, cols])


def _rwkv_scan_kernel(n_chain, *refs):
    ins, y_refs, h_ref = refs[:4 * n_chain], refs[4 * n_chain:5 * n_chain], refs[-1]
    nl = RW_G // HD_C

    @pl.when(pl.program_id(0) == 0)
    def _():
        h_ref[...] = jnp.zeros_like(h_ref)

    r4 = lax.broadcasted_iota(jnp.int32, (RW_G, RW_G), 0) // HD_C
    c4 = lax.broadcasted_iota(jnp.int32, (RW_G, RW_G), 1) // HD_C
    dmask = r4 == c4
    n_chunk = TM // RW_L
    links = [(j, slice(gi * RW_G, (gi + 1) * RW_G)) for j in range(n_chain)
             for gi in range(W_GRP // RW_G)]
    h = [h_ref[j, :, cols] for j, cols in links]
    for s in range(n_chunk):
        prods = []
        for (j, cols), hj in zip(links, h):
            g_ref, _, rp_ref, _ = ins[4 * j:4 * j + 4]
            c = n_chunk - 1 - s if j % 2 else s
            rows = slice(c * RW_L, (c + 1) * RW_L)
            h4 = jnp.where(dmask, jnp.concatenate([hj] * nl, axis=0), 0.0)
            hh, hl = _split_bf16(h4)
            lhs = jnp.concatenate([g_ref[0, 0, rows, cols], rp_ref[0, 0, rows, cols]], axis=0)
            lh, ll = _split_bf16(lhs)
            prods.append(_mm(lh, hh) + _mm(lh, hl) + _mm(ll, hh))
        for k, ((j, cols), prod) in enumerate(zip(links, prods)):
            _, q_ref, _, yp_ref = ins[4 * j:4 * j + 4]
            c = n_chunk - 1 - s if j % 2 else s
            rows = slice(c * RW_L, (c + 1) * RW_L)
            y_refs[j][0, 0, rows, cols] = prod[RW_L:] + yp_ref[0, 0, rows, cols]
            h[k] = prod[:RW_L] + q_ref[0, 0, rows, cols]
    for (j, cols), hj in zip(links, h):
        h_ref[j, :, cols] = hj


def _rwkv_merge_kernel(n_batch, *refs):
    y_refs = refs[:2 * n_batch]
    bonus_ref, g_ref, lw_ref, lb_ref, o_ref = refs[2 * n_batch:]
    ri = lax.broadcasted_iota(jnp.int32, (W_GRP, W_GRP), 0) // HD_C
    ci = lax.broadcasted_iota(jnp.int32, (W_GRP, W_GRP), 1) // HD_C
    ones_bd = jnp.where(ri == ci, 1.0, 0.0).astype(BF16)
    for bb in range(n_batch):
        y = y_refs[2 * bb][0, 0] + y_refs[2 * bb + 1][0, 0]
        mean = _seg_sum(y, ones_bd) * (1.0 / HD_C)
        yc = y - mean
        var = _seg_sum(yc * yc, ones_bd) * (1.0 / HD_C)
        yn = yc * lax.rsqrt(var + GN_EPS) * lw_ref[...] + lb_ref[...]
        out = yn + bonus_ref[bb, 0] + bonus_ref[bb, 1]
        o_ref[bb] = (out * _silu(g_ref[bb].astype(F32))).astype(BF16)


def _scan_block(dr, i, n_blk):
    return jnp.where(dr == 0, i, jnp.where(i == 0, 0, n_blk - i))


def _rwkv(pc, p):
    b, t, _ = pc.shape
    nblk = t // TM
    hb = TM // 16
    lblk = 4 * W_GRP // LORA_PAD

    def halo_idx(dr, i):
        return jnp.where(dr == 0, jnp.maximum(i * hb - 1, 0), jnp.minimum((i + 1) * hb, nblk * hb - 1))

    per_dir = lambda n: pl.BlockSpec((1, 1, n), lambda bb, dr, i: (dr, 0, 0))
    per_dir_m = lambda m, n: pl.BlockSpec((1, m, n), lambda bb, dr, i: (dr, 0, 0))
    state = lambda: pl.BlockSpec((1, 1, TM, W_GRP), lambda bb, dr, i: (bb, dr, i, 0))
    sds = jax.ShapeDtypeStruct((b, 2, t, W_GRP), F32)
    gm, qm, rp, yp, bonus = pl.pallas_call(
        functools.partial(_rwkv_prep_kernel, nblk),
        grid=(b, 2, nblk),
        in_specs=[pl.BlockSpec((1, TM, 3 * W_GRP), lambda bb, dr, i: (bb, i, 0)),
                  pl.BlockSpec((1, TM, LORA_PAD), lambda bb, dr, i: (bb, i, lblk + dr)),
                  pl.BlockSpec((1, 16, 3 * W_GRP), lambda bb, dr, i: (bb, halo_idx(dr, i), 0)),
                  pl.BlockSpec((1, 16, LORA_PAD), lambda bb, dr, i: (bb, halo_idx(dr, i), lblk + dr)),
                  per_dir(3 * W_GRP), per_dir(LORA_PAD), per_dir(W_GRP), per_dir_m(LORA_PAD, W_GRP),
                  per_dir(W_GRP), per_dir_m(LORA_PAD, W_GRP), per_dir(W_GRP), per_dir(W_GRP),
                  per_dir(W_GRP)],
        out_specs=[state()] * 5,
        out_shape=[sds] * 5,
        compiler_params=_params(("parallel", "parallel", "parallel")),
        name="rwkv_prep",
    )(pc, pc, pc, pc, p["mu_rkv"], p["mu_lora"], p["w0"], p["w2"], p["a0"], p["a2"],
      p["kk"], p["ka"], p["rk"])

    chains = [(bb, dr) for bb in range(b) for dr in range(2)]
    scan_spec = lambda bb, dr: pl.BlockSpec(
        (1, 1, TM, W_GRP), lambda i: (bb, dr, _scan_block(dr, i, nblk), 0))
    sds1 = jax.ShapeDtypeStruct((1, 1, t, W_GRP), F32)
    ys = pl.pallas_call(
        functools.partial(_rwkv_scan_kernel, len(chains)),
        grid=(nblk,),
        in_specs=[scan_spec(bb, dr) for bb, dr in chains for _ in range(4)],
        out_specs=[pl.BlockSpec((1, 1, TM, W_GRP), lambda i, dr=dr: (0, 0, _scan_block(dr, i, nblk), 0))
                   for _, dr in chains],
        out_shape=[sds1] * len(chains),
        scratch_shapes=[pltpu.VMEM((len(chains), RW_L, W_GRP), F32)],
        compiler_params=_params(("arbitrary",)),
        name="rwkv_scan",
    )(*[a for _ in chains for a in (gm, qm, rp, yp)])

    return pl.pallas_call(
        functools.partial(_rwkv_merge_kernel, b),
        grid=(nblk,),
        in_specs=[pl.BlockSpec((1, 1, TM, W_GRP), lambda i: (0, 0, i, 0))] * len(chains) + [
                  pl.BlockSpec((b, 2, TM, W_GRP), lambda i: (0, 0, i, 0)),
                  pl.BlockSpec((b, TM, W_GRP), lambda i: (0, i, 3)),
                  pl.BlockSpec((1, W_GRP), lambda i: (0, 0)),
                  pl.BlockSpec((1, W_GRP), lambda i: (0, 0))],
        out_specs=pl.BlockSpec((b, TM, W_GRP), lambda i: (0, i, 0)),
        out_shape=jax.ShapeDtypeStruct((b, t, W_GRP), BF16),
        compiler_params=_params(("parallel",)),
        name="rwkv_merge",
    )(*ys, bonus, pc, p["ln_w"], p["ln_b"])


def _outproj_kernel(mod_row, oa_ref, ob_ref, oc_ref, od_ref, w_ref, x_ref, mod_ref, g_ref, o_ref):
    d = x_ref.shape[-1]
    b = pl.program_id(0)
    acc = _mm(oa_ref[0], w_ref[0, 0:W_GRP, :])
    acc = acc + _mm(ob_ref[0], w_ref[0, W_GRP:2 * W_GRP, :])
    acc = acc + _mm(oc_ref[0], w_ref[0, 2 * W_GRP:3 * W_GRP, :])
    acc = acc + _mm(od_ref[0], w_ref[0, 3 * W_GRP:, :])
    y = acc * lax.rsqrt(jnp.mean(acc * acc, axis=-1, keepdims=True) + EPS) * g_ref[...]
    row = mod_ref[0, pl.ds(b if mod_row is None else mod_row, 1), :]
    o_ref[0] = x_ref[0] + row[:, 2 * d:] * y


def _outproj(mix, w_out, x, mods, layer, g_post, blk0, mod_row):
    b, n, d = x.shape
    o_spec = lambda off: pl.BlockSpec((1, TM, W_GRP), lambda bb, i: (bb, i + off, 0))
    return pl.pallas_call(
        functools.partial(_outproj_kernel, mod_row),
        grid=(b, n // TM),
        in_specs=[o_spec(off) for off in blk0] + [
            pl.BlockSpec((1,) + w_out.shape[1:], lambda bb, i: (layer, 0, 0)),
            pl.BlockSpec((1, TM, d), lambda bb, i: (bb, i, 0)),
            pl.BlockSpec((1, 8, 3 * d), lambda bb, i: (layer, 0, 0)),
            pl.BlockSpec((1, d), lambda bb, i: (0, 0))],
        out_specs=pl.BlockSpec((1, TM, d), lambda bb, i: (bb, i, 0)),
        out_shape=jax.ShapeDtypeStruct((b, n, d), F32),
        compiler_params=_params(("parallel", "parallel")),
        name="outproj",
    )(*mix, w_out, x, mods, g_post.reshape(1, d))


def _w_in_segments():
    g = W_GRP
    o = [0]
    for s in (g, g, g, g, g, g, g, 3 * g, LORA_W + LORA_A, LORA_W + LORA_A, g, g, g, g, g):
        o.append(o[-1] + s)
    order = [0, 1, 2, 3, 4, 5, 6, 12, 13, 11, 14, 7, 10, 8, 9]
    segs, dst = [], 0
    for j in order:
        width = o[j + 1] - o[j]
        segs.append((o[j], width, dst))
        dst += -(-width // LORA_PAD) * LORA_PAD
    assert dst == N_ALL
    return segs, o[-1]


def _w_in_kernel(wt_ref, o_ref):
    segs, _ = _w_in_segments()
    cols = wt_ref.shape[2]
    for src, width, dst in segs:
        piece = wt_ref[0, src:src + width, :]
        padded = -(-width // LORA_PAD) * LORA_PAD
        if padded != width:
            piece = jnp.concatenate([piece, jnp.zeros((padded - width, cols), F32)], axis=0)
        o_ref[0, :, dst:dst + padded] = piece.T.astype(BF16)


def _regroup_w_in(w_in):
    depth, d, n_in = w_in.shape
    assert n_in == _w_in_segments()[1]
    tr = 256
    return pl.pallas_call(
        _w_in_kernel,
        grid=(depth, d // tr),
        in_specs=[pl.BlockSpec((1, n_in, tr), lambda l, i: (l, 0, i))],
        out_specs=pl.BlockSpec((1, tr, N_ALL), lambda l, i: (l, i, 0)),
        out_shape=jax.ShapeDtypeStruct((depth, d, N_ALL), BF16),
        compiler_params=_params(("parallel", "parallel")),
        name="w_in_prep",
    )(jnp.swapaxes(w_in, 1, 2))


def _cast_kernel(w_ref, o_ref):
    o_ref[...] = w_ref[...].astype(BF16)


def _cast_w_out(w_out):
    depth, k, d = w_out.shape
    tr = 512
    return pl.pallas_call(
        _cast_kernel,
        grid=(depth, k // tr),
        in_specs=[pl.BlockSpec((1, tr, d), lambda l, i: (l, i, 0))],
        out_specs=pl.BlockSpec((1, tr, d), lambda l, i: (l, i, 0)),
        out_shape=jax.ShapeDtypeStruct((depth, k, d), BF16),
        compiler_params=_params(("parallel", "parallel")),
        name="w_out_prep",
    )(w_out)


def _rope_tables(n_ctx, n_lat):
    n_rows = n_lat // GRID_W
    inv = ROPE_BASE ** (-jnp.arange(0, 32, 2, dtype=F32) / 32)
    lane = jnp.arange(HD_B)
    use_col = ((lane // 32) % 2 == 1)[None, None, :]
    trig = lambda f, n: f(jnp.arange(n, dtype=F32)[:, None] * inv[lane % 16][None, :])
    grid = lambda f: jnp.where(use_col, trig(f, GRID_W)[None, :, :], trig(f, n_rows)[:, None, :]
                               ).reshape(n_lat, HD_B)
    first = (lane % 32 < 16)[None, :]
    cos, sin = grid(jnp.cos), grid(jnp.sin)
    sa = jnp.where(first, -sin, 0.0)
    sb = jnp.where(first, 0.0, sin)
    ident = lambda v: jnp.full((n_ctx, HD_B), v, F32)
    return (jnp.concatenate([ident(1.0), cos]), jnp.concatenate([ident(0.0), sa]),
            jnp.concatenate([ident(0.0), sb]))


def _rwkv_params(l, mu, w0, w2, a0, a2, kk, ka, rk, ln_w, ln_b):
    g = W_GRP
    zpad = lambda n: jnp.zeros((2, n), F32)
    mu_l = jnp.concatenate([mu[l][:, 3 * g:], zpad(LORA_PAD - LORA_W - LORA_A)], axis=1)
    w2p = jnp.concatenate([w2[l], jnp.zeros((2, LORA_PAD - LORA_W, g), F32)], axis=1)
    a2p = jnp.concatenate([jnp.zeros((2, LORA_W, g), F32), a2[l],
                           jnp.zeros((2, LORA_PAD - LORA_W - LORA_A, g), F32)], axis=1)
    r3 = lambda v: v.reshape(2, 1, -1)
    return dict(mu_rkv=r3(mu[l][:, :3 * g]), mu_lora=r3(mu_l), w0=r3(w0[l]), w2=w2p, a0=r3(a0[l]),
                a2=a2p, kk=r3(kk[l]), ka=r3(ka[l]), rk=r3(rk[l].reshape(2, g)),
                ln_w=ln_w[l].reshape(1, g), ln_b=ln_b[l].reshape(1, g))


def kernel(x, c, ctx, c_ctx, w_mod, b_mod, g_pre, g_post, w_in, w_out, sgu_w, sgu_b, lam_q1, lam_k1, lam_q2, lam_k2, subln_g, rwkv_mu, rwkv_w0, rwkv_w2, rwkv_a0, rwkv_a2, rwkv_kk, rwkv_ka, rwkv_rk, rwkv_ln_w, rwkv_ln_b, conv_w):
    depth = w_mod.shape[0]
    b, n_lat, _ = x.shape
    n_ctx = ctx.shape[1]
    mods = _modulation(c, c_ctx, w_mod, b_mod)
    rope = _rope_tables(n_ctx, n_lat)
    w_in_b = _regroup_w_in(w_in)
    w_out_b = _cast_w_out(w_out)
    xc = ctx
    for l in range(depth):
        need_ctx = l < depth - 1
        lam_init = 0.8 - 0.6 * math.exp(-0.3 * l)
        pa, pb, pd, pc = _inproj(xc, x, mods, l, g_pre[l], w_in_b, rope)
        o_a, o_d = _local_mixers(pa, pd, sgu_w[l], sgu_b[l], conv_w[l])
        lam_params = jnp.stack([lam_q1[l], lam_k1[l], lam_q2[l], lam_k2[l]])
        cb = n_ctx // TM
        o_b = _attention(pb, lam_params, subln_g[l], lam_init, cb, n_lat, n_ctx + n_lat, ATT_NSUB)
        o_c = _rwkv(pc, _rwkv_params(l, rwkv_mu, rwkv_w0, rwkv_w2, rwkv_a0, rwkv_a2, rwkv_kk,
                                     rwkv_ka, rwkv_rk, rwkv_ln_w, rwkv_ln_b))
        x_new = _outproj((o_a, o_b, o_c, o_d), w_out_b, x, mods, l, g_post[l], (cb, 0, cb, cb), None)
        if need_ctx:
            o_bc = _attention(pb, lam_params, subln_g[l], lam_init, 0, n_ctx, n_ctx, 1)
            xc = _outproj((o_a, o_bc, o_c, o_d), w_out_b, xc, mods, l, g_post[l], (0, 0, 0, 0), b)
        x = x_new
    return x
```

```python
import functools
import math

import jax
import jax.numpy as jnp
from jax import lax
from jax.experimental import pallas as pl
from jax.experimental.pallas import tpu as pltpu

F32 = jnp.float32
BF16 = jnp.bfloat16

EPS = 1e-6
GN_EPS = 64e-5
GRID_W = 64
ROPE_BASE = 10000.0
LOG2E = 1.4426950408889634

W_GRP = 512
CHUNK_A = 128
HD_B = 128
HD_QK = 64
HD_C = 64
LORA_W = 64
LORA_A = 32
LORA_PAD = 128
RW_L = 64
RW_G = 256
TM = 256
ATT_KB = 512
ATT_NSUB = 4
ATT_CHAIN = 2
VMEM_LIMIT = 56 * 1024 * 1024

N_A = 3 * W_GRP
N_B = 4 * W_GRP
N_D = 4 * W_GRP
N_C = 4 * W_GRP + 2 * LORA_PAD
OFF_A, OFF_B, OFF_D, OFF_C = 0, N_A, N_A + N_B, N_A + N_B + N_D
N_ALL = OFF_C + N_C


def _silu(g):
    return g * (1.0 / (1.0 + jnp.exp(-g)))


def _sigmoid(x):
    return 1.0 / (1.0 + jnp.exp(-x))


def _split_bf16(x):
    hi = x.astype(BF16)
    lo = (x - hi.astype(F32)).astype(BF16)
    return hi, lo


def _mm(a, b):
    return jnp.dot(a, b, preferred_element_type=F32)


def _mm_nt(a, b):
    return lax.dot_general(a, b, (((1,), (1,)), ((), ())), preferred_element_type=F32)


def _params(sem):
    return pltpu.CompilerParams(dimension_semantics=sem, vmem_limit_bytes=VMEM_LIMIT)


def _mod_kernel(s_ref, w_ref, b_ref, o_ref):
    s = _silu(s_ref[...])
    o_ref[0] = _mm(s.astype(BF16), w_ref[0].astype(BF16)) + b_ref[0]


def _modulation(c, c_ctx, w_mod, b_mod):
    depth, d, n3 = w_mod.shape
    b = c.shape[0]
    rows = jnp.zeros((8, d), F32).at[:b].set(c).at[b].set(c_ctx)
    tn = 768
    return pl.pallas_call(
        _mod_kernel,
        grid=(depth, n3 // tn),
        in_specs=[pl.BlockSpec((8, d), lambda l, j: (0, 0)),
                  pl.BlockSpec((1, d, tn), lambda l, j: (l, 0, j)),
                  pl.BlockSpec((1, 1, tn), lambda l, j: (l, 0, j))],
        out_specs=pl.BlockSpec((1, 8, tn), lambda l, j: (l, 0, j)),
        out_shape=jax.ShapeDtypeStruct((depth, 8, n3), F32),
        compiler_params=_params(("parallel", "parallel")),
        name="modulation",
    )(rows, w_mod, b_mod.reshape(depth, 1, n3))


def _inproj_kernel(n_batch, xc_ref, xl_ref, mod_ref, g_ref, w_ref, cos_ref, sa_ref, sb_ref,
                   pa_ref, pb_ref, pd_ref, pc_ref):
    b = pl.program_id(0)
    i = pl.program_id(1)
    d = xc_ref.shape[-1]
    is_ctx = i == 0
    x = jnp.where(is_ctx, xc_ref[0], xl_ref[0])
    y = x * lax.rsqrt(jnp.mean(x * x, axis=-1, keepdims=True) + EPS) * g_ref[...]
    row = jnp.where(is_ctx, mod_ref[0, pl.ds(n_batch, 1), :], mod_ref[0, pl.ds(b, 1), :])
    h = (y * (1.0 + row[:, d:2 * d]) + row[:, :d]).astype(BF16)

    def proj(c0, c1):
        return _mm(h, w_ref[0, :, c0:c1])

    pa_ref[0] = proj(OFF_A, OFF_A + N_A).astype(BF16)
    cos, sa, sb = cos_ref[...], sa_ref[...], sb_ref[...]
    qk = proj(OFF_B, OFF_B + 2 * W_GRP)
    for j in range(2 * W_GRP // HD_B):
        t = qk[:, j * HD_B:(j + 1) * HD_B]
        t = t * cos + pltpu.roll(t, HD_B - 16, axis=1) * sa + pltpu.roll(t, 16, axis=1) * sb
        if j < W_GRP // HD_B:
            t = t * (HD_QK ** -0.5 * LOG2E)
        pb_ref[0, :, j * HD_B:(j + 1) * HD_B] = t.astype(BF16)
    pb_ref[0, :, 2 * W_GRP:] = proj(OFF_B + 2 * W_GRP, OFF_B + N_B).astype(BF16)
    pd_ref[0] = proj(OFF_D, OFF_D + N_D).astype(BF16)
    pc_ref[0] = proj(OFF_C, OFF_C + N_C).astype(BF16)


def _inproj(xc, xl, mods, layer, g_pre, w_in, rope):
    b, n_ctx, d = xc.shape
    n_lat = xl.shape[1]
    t = n_ctx + n_lat
    assert n_ctx == TM and n_lat % TM == 0
    nblk = t // TM
    cos, sa, sb = rope
    tab = pl.BlockSpec((TM, HD_B), lambda bb, i: (i, 0))
    out = lambda n: pl.BlockSpec((1, TM, n), lambda bb, i: (bb, i, 0))
    return pl.pallas_call(
        functools.partial(_inproj_kernel, b),
        grid=(b, nblk),
        in_specs=[pl.BlockSpec((1, TM, d), lambda bb, i: (bb, 0, 0)),
                  pl.BlockSpec((1, TM, d), lambda bb, i: (bb, jnp.maximum(i - 1, 0), 0)),
                  pl.BlockSpec((1, 8, 3 * d), lambda bb, i: (layer, 0, 0)),
                  pl.BlockSpec((1, d), lambda bb, i: (0, 0)),
                  pl.BlockSpec((1, d, N_ALL), lambda bb, i: (layer, 0, 0), pipeline_mode=pl.Buffered(1)),
                  tab, tab, tab],
        out_specs=[out(N_A), out(N_B), out(N_D), out(N_C)],
        out_shape=[jax.ShapeDtypeStruct((b, t, n), BF16) for n in (N_A, N_B, N_D, N_C)],
        compiler_params=_params(("parallel", "arbitrary")),
        name="inproj",
    )(xc, xl, mods, g_pre.reshape(1, d), w_in, cos, sa, sb)


def _local_kernel(n_blk, pa_ref, pd_ref, hp_ref, hn_ref, ws_ref, bs_ref, cw_ref, oa_ref, od_ref):
    i = pl.program_id(1)
    for c in range(TM // CHUNK_A):
        rows = slice(c * CHUNK_A, (c + 1) * CHUNK_A)
        for h in range(W_GRP // CHUNK_A):
            cols = slice(h * CHUNK_A, (h + 1) * CHUNK_A)
            u = pa_ref[0, rows, cols].astype(F32)
            v = pa_ref[0, rows, W_GRP + h * CHUNK_A:W_GRP + (h + 1) * CHUNK_A].astype(F32)
            g = pa_ref[0, rows, 2 * W_GRP + h * CHUNK_A:2 * W_GRP + (h + 1) * CHUNK_A].astype(F32)
            mean = jnp.mean(v, axis=-1, keepdims=True)
            vc = v - mean
            vn = vc * lax.rsqrt(jnp.mean(vc * vc, axis=-1, keepdims=True) + EPS)
            mixed = _mm(ws_ref[h].astype(BF16), vn.astype(BF16)) + bs_ref[:, h:h + 1]
            oa_ref[0, rows, cols] = (u * mixed * _silu(g)).astype(BF16)
    cx = pd_ref[0, :, 0:2 * W_GRP].astype(F32)
    z = cx[:, :W_GRP] * cx[:, W_GRP:]
    hp = hp_ref[0].astype(F32)
    hn = hn_ref[0].astype(F32)
    zp = hp[15:16, :W_GRP] * hp[15:16, W_GRP:]
    zn = hn[0:1, :W_GRP] * hn[0:1, W_GRP:]
    zp = jnp.where(i <= 1, 0.0, zp)
    zn = jnp.where(jnp.logical_or(i == 0, i == n_blk - 1), 0.0, zn)
    r = lax.broadcasted_iota(jnp.int32, (TM, 1), 0)
    z_prev = jnp.where(r == 0, zp, pltpu.roll(z, 1, axis=0))
    z_next = jnp.where(r == TM - 1, zn, pltpu.roll(z, TM - 1, axis=0))
    w = cw_ref[...]
    y = z_prev * w[0:1] + z * w[1:2] + z_next * w[2:3]
    bg = pd_ref[0, :, 2 * W_GRP:3 * W_GRP].astype(F32)
    g = pd_ref[0, :, 3 * W_GRP:].astype(F32)
    od_ref[0] = (bg * y * _silu(g)).astype(BF16)


def _local_mixers(pa, pd, sgu_w, sgu_b, conv_w):
    b, t, _ = pa.shape
    nblk = t // TM
    hb = TM // 16
    return pl.pallas_call(
        functools.partial(_local_kernel, nblk),
        grid=(b, nblk),
        in_specs=[pl.BlockSpec((1, TM, N_A), lambda bb, i: (bb, i, 0)),
                  pl.BlockSpec((1, TM, N_D), lambda bb, i: (bb, i, 0)),
                  pl.BlockSpec((1, 16, 2 * W_GRP), lambda bb, i: (bb, jnp.maximum(i * hb - 1, 0), 0)),
                  pl.BlockSpec((1, 16, 2 * W_GRP),
                               lambda bb, i: (bb, jnp.minimum((i + 1) * hb, nblk * hb - 1), 0)),
                  pl.BlockSpec(sgu_w.shape, lambda bb, i: (0, 0, 0)),
                  pl.BlockSpec((CHUNK_A, sgu_w.shape[0]), lambda bb, i: (0, 0)),
                  pl.BlockSpec(conv_w.shape, lambda bb, i: (0, 0))],
        out_specs=[pl.BlockSpec((1, TM, W_GRP), lambda bb, i: (bb, i, 0))] * 2,
        out_shape=[jax.ShapeDtypeStruct((b, t, W_GRP), BF16)] * 2,
        compiler_params=_params(("parallel", "parallel")),
        name="local_mixers",
    )(pa, pd, pd, pd, sgu_w, sgu_b.T, conv_w)


def _attn_kernel(lam_init, nsub, *refs):
    q_refs, (k_ref, v_ref), g_refs = refs[:nsub], refs[nsub:nsub + 2], refs[nsub + 2:2 * nsub + 2]
    lam_ref, sg_ref, o_ref, vext_ref = refs[2 * nsub + 2:]
    tq = nsub * TM
    n_keys = k_ref.shape[1]

    @pl.when(pl.program_id(2) == 0)
    def _():
        vext_ref[:, :HD_B] = v_ref[0]
        vext_ref[:, HD_B:] = jnp.ones((n_keys, HD_B), BF16)

    lp = lam_ref[...]
    lam = (jnp.exp(jnp.sum(lp[0:1] * lp[1:2], axis=-1, keepdims=True))
           - jnp.exp(jnp.sum(lp[2:3] * lp[3:4], axis=-1, keepdims=True)) + lam_init)
    lane = lax.broadcasted_iota(jnp.int32, (1, HD_B), 1)
    n_chain = -(-nsub // ATT_CHAIN)
    tc = tq // n_chain
    q2s = []
    for ci in range(n_chain):
        q = jnp.concatenate([r[0] for r in q_refs[ci * ATT_CHAIN:(ci + 1) * ATT_CHAIN]], axis=0)
        zero = jnp.zeros_like(q)
        q2s.append(jnp.concatenate([jnp.where(lane < HD_QK, q, zero),
                                    jnp.where(lane >= HD_QK, q, zero)], axis=0))
    ms = [None] * n_chain
    os_ = [None] * n_chain
    for k0 in range(0, n_keys, ATT_KB):
        k1 = min(k0 + ATT_KB, n_keys)
        for ci in range(n_chain):
            s = _mm_nt(q2s[ci], k_ref[0, k0:k1])
            bm = jnp.max(s, axis=-1, keepdims=True)
            if ms[ci] is None:
                ms[ci] = bm
                os_[ci] = _mm(jnp.exp2(s - bm).astype(BF16), vext_ref[k0:k1])
            else:
                m_new = jnp.maximum(ms[ci], bm)
                os_[ci] = (os_[ci] * jnp.exp2(ms[ci] - m_new)
                           + _mm(jnp.exp2(s - m_new).astype(BF16), vext_ref[k0:k1]))
                ms[ci] = m_new
    for ci in range(n_chain):
        o = os_[ci]
        a = o[:tc, :HD_B] / o[:tc, HD_B:] - lam * (o[tc:, :HD_B] / o[tc:, HD_B:])
        y = a * lax.rsqrt(jnp.mean(a * a, axis=-1, keepdims=True) + EPS) * sg_ref[...] * (1.0 - lam_init)
        g = jnp.concatenate([r[0] for r in g_refs[ci * ATT_CHAIN:(ci + 1) * ATT_CHAIN]], axis=0)
        o_ref[0, ci * tc:(ci + 1) * tc, :] = (y * _silu(g.astype(F32))).astype(BF16)


def _attention(pb, lam_params, subln_g, lam_init, q_blk0, n_q, n_keys, nsub):
    b = pb.shape[0]
    nh = W_GRP // HD_B
    assert n_q % (nsub * TM) == 0
    q_spec = lambda col0, j: pl.BlockSpec(
        (1, TM, HD_B), lambda bb, h, i: (bb, nsub * i + q_blk0 + j, col0 + h))
    kv_spec = lambda col0: pl.BlockSpec((1, n_keys, HD_B), lambda bb, h, i: (bb, 0, col0 + h))
    return pl.pallas_call(
        functools.partial(_attn_kernel, lam_init, nsub),
        grid=(b, nh, n_q // (nsub * TM)),
        in_specs=([q_spec(0, j) for j in range(nsub)] + [kv_spec(nh), kv_spec(2 * nh)]
                  + [q_spec(3 * nh, j) for j in range(nsub)]
                  + [pl.BlockSpec((4, HD_QK), lambda bb, h, i: (0, 0)),
                     pl.BlockSpec((1, HD_B), lambda bb, h, i: (0, 0))]),
        out_specs=pl.BlockSpec((1, nsub * TM, HD_B), lambda bb, h, i: (bb, i, h)),
        out_shape=jax.ShapeDtypeStruct((b, n_q, W_GRP), BF16),
        scratch_shapes=[pltpu.VMEM((n_keys, 2 * HD_B), BF16)],
        compiler_params=_params(("parallel", "parallel", "arbitrary")),
        name="diff_attention",
    )(*([pb] * (2 * nsub + 2)), lam_params, subln_g.reshape(1, HD_B))


def _seg_sum(x, ones_bd):
    hi, lo = _split_bf16(x)
    return _mm(hi, ones_bd) + _mm(lo, ones_bd)


def _head_diag(x4, mask):
    return jnp.where(mask, x4, 0.0).astype(BF16)


def _rwkv_prep_kernel(n_blk, rkv_ref, lo_ref, hrkv_ref, hlo_ref, mu_ref, mul_ref, w0_ref, w2_ref,
                      a0_ref, a2_ref, kk_ref, ka_ref, rk_ref,
                      g_ref, q_ref, rp_ref, yp_ref, bonus_ref):
    dr = pl.program_id(1)
    i = pl.program_id(2)
    fwd = dr == 0
    nl = RW_G // HD_C
    sgn = 1 - 2 * dr
    hrow = jnp.where(fwd, 15, 0)
    edge = jnp.logical_or(i == 0, i == jnp.where(fwd, 1, n_blk - 1))
    rsel = lax.broadcasted_iota(jnp.int32, (TM, 1), 0) == jnp.where(fwd, 0, TM - 1)
    hsel = lax.broadcasted_iota(jnp.int32, (16, 1), 0) == hrow

    def shifted(cur_ref, halo_ref, mu):
        f = cur_ref[0].astype(F32)
        halo = jnp.sum(jnp.where(hsel, halo_ref[0].astype(F32), 0.0), axis=0, keepdims=True)
        halo = jnp.where(edge, 0.0, halo)
        rolled = jnp.where(fwd, pltpu.roll(f, 1, axis=0), pltpu.roll(f, TM - 1, axis=0))
        prev = jnp.where(rsel, halo, rolled)
        return f + (prev - f) * mu

    z = shifted(rkv_ref, hrkv_ref, mu_ref[0])
    zl = shifted(lo_ref, hlo_ref, mul_ref[0])
    r, k, v = z[:, :W_GRP], z[:, W_GRP:2 * W_GRP], z[:, 2 * W_GRP:]
    wl = w0_ref[0] + _mm(jnp.tanh(zl).astype(BF16), w2_ref[0].astype(BF16))
    logd = -math.exp(-0.5) * _sigmoid(wl)
    a = _sigmoid(a0_ref[0] + _mm(zl.astype(BF16), a2_ref[0].astype(BF16)))
    ri = lax.broadcasted_iota(jnp.int32, (W_GRP, W_GRP), 0) // HD_C
    ci = lax.broadcasted_iota(jnp.int32, (W_GRP, W_GRP), 1) // HD_C
    ones_bd = jnp.where(ri == ci, 1.0, 0.0).astype(BF16)
    kk = k * kk_ref[0]
    kk = kk / jnp.maximum(jnp.sqrt(_seg_sum(kk * kk, ones_bd)), 1e-12)
    k2 = k * (1.0 + (a - 1.0) * ka_ref[0])
    bv = kk * a
    bonus_ref[0, 0] = (_seg_sum(r * k2 * rk_ref[0], ones_bd) * v).astype(BF16)
    rt = lax.broadcasted_iota(jnp.int32, (TM, TM), 0)
    ct = lax.broadcasted_iota(jnp.int32, (TM, TM), 1)
    same = rt // RW_L == ct // RW_L
    tri = jnp.where(jnp.logical_and(same, (rt - ct) * sgn >= 0), 1.0, 0.0).astype(BF16)
    blk = jnp.where(same, 1.0, 0.0).astype(BF16)
    lh, ll = _split_bf16(logd)
    cum = _mm(tri, lh) + _mm(tri, ll)
    tot = _mm(blk, lh) + _mm(blk, ll)
    half = 0.5 * tot
    e_pos = jnp.exp(cum - half)
    e_neg = jnp.exp(half - cum)
    e_prev = jnp.exp(cum - logd - half)
    e_end = jnp.exp(tot - cum)
    g_end = jnp.exp(tot)
    e_half = jnp.exp(half)
    kt_all = kk * e_prev
    rt_all = r * e_pos
    kh_all = k2 * e_neg
    bh_all = bv * e_neg
    kb_all = k2 * e_end
    bb_all = bv * e_end
    row = lax.broadcasted_iota(jnp.int32, (RW_L, RW_G), 0)
    col = lax.broadcasted_iota(jnp.int32, (RW_L, RW_G), 1) % HD_C
    eye = jnp.where(col == row, 1.0, 0.0)
    r4 = lax.broadcasted_iota(jnp.int32, (RW_G, RW_G), 0) // HD_C
    c4 = lax.broadcasted_iota(jnp.int32, (RW_G, RW_G), 1) // HD_C
    dmask = r4 == c4
    r2 = lax.broadcasted_iota(jnp.int32, (2 * RW_L, RW_G), 0) % RW_L
    c2 = lax.broadcasted_iota(jnp.int32, (2 * RW_L, RW_G), 1) % HD_C
    first = jnp.where(lax.broadcasted_iota(jnp.int32, (2 * RW_L, RW_G), 0) < RW_L, 1, 0)
    mask2 = (r2 - c2) * sgn >= first

    def bd(x):
        return _head_diag(jnp.concatenate([x] * nl, axis=0), dmask)

    def hprod(x, y):
        return _mm(x.astype(BF16), bd(y))

    def diag_blocks(full):
        out = jnp.zeros((RW_L, RW_G), F32)
        lane_h = lax.broadcasted_iota(jnp.int32, (RW_L, RW_G), 1) // HD_C
        for hh in range(nl):
            out = out + jnp.where(lane_h == hh, full[hh * HD_C:(hh + 1) * HD_C, :], 0.0)
        return out

    pairs = [(slice(c * RW_L, (c + 1) * RW_L), slice(gi * RW_G, (gi + 1) * RW_G))
             for c in range(TM // RW_L) for gi in range(W_GRP // RW_G)]
    cat = lambda *xs: jnp.concatenate(xs, axis=0)
    L = RW_L
    kt = [kt_all[p] for p in pairs]
    rtl = [rt_all[p] for p in pairs]
    vv = [v[p] for p in pairs]
    lhs = [cat(a_, b_).astype(BF16) for a_, b_ in zip(kt, rtl)]
    pb_ = [jnp.where(mask2, _mm_nt(x, bd(bh_all[p])), 0.0) for x, p in zip(lhs, pairs)]
    pk_ = [jnp.where(mask2, _mm_nt(x, bd(kh_all[p])), 0.0) for x, p in zip(lhs, pairs)]
    res = [hprod(x, x[:L]) for x in pb_]
    tinv = [eye - x[:L] for x in pb_]
    mt = [x[L:] - y[L:] for x, y in zip(pb_, res)]
    pw = [y[:L] for y in res]
    for _ in range(4):
        res = [hprod(cat(t_, m_, w_), w_) for t_, m_, w_ in zip(tinv, mt, pw)]
        tinv = [t_ + y[:L] for t_, y in zip(tinv, res)]
        mt = [m_ + y[L:2 * L] for m_, y in zip(mt, res)]
        pw = [y[2 * L:] for y in res]
    res = [hprod(cat(t_, m_), w_) for t_, m_, w_ in zip(tinv, mt, pw)]
    tm = [cat(t_, m_) + y for t_, m_, y in zip(tinv, mt, res)]
    wv = [hprod(x, y) for x, y in zip(pk_, vv)]
    tmb = [x.astype(BF16) for x in tm]
    ww = [_mm(x, bd(y)) for x, y in zip(tmb, kt)]
    uu = [_mm(x, bd(y[:L])) for x, y in zip(tmb, wv)]
    for j, p in enumerate(pairs):
        rows, cols = p
        eh = e_half[rows.start:rows.start + 1, cols]
        rp_ref[0, 0, rows, cols] = ((rtl[j] - ww[j][L:]) * eh).astype(BF16)
        yp_ref[0, 0, rows, cols] = (wv[j][L:] - uu[j][L:]).astype(BF16)
        xt = cat(kb_all[p], -bb_all[p]).T.astype(BF16)
        zeros = jnp.zeros((L, RW_G), F32)
        yq = cat(vv[j], uu[j][:L]).astype(BF16)
        yg = cat(zeros, ww[j][:L]).astype(BF16)
        q_ref[0, 0, rows, cols] = diag_blocks(_mm(xt, yq))
        g_ref[0, 0, rows, cols] = (diag_blocks(_mm(xt, yg)) * eh
                                   + eye * g_end[rows.start:rows.start + 1, cols])


def _rwkv_scan_kernel(n_chain, *refs):
    ins, y_refs, h_ref = refs[:4 * n_chain], refs[4 * n_chain:5 * n_chain], refs[-1]
    nl = RW_G // HD_C

    @pl.when(pl.program_id(0) == 0)
    def _():
        h_ref[...] = jnp.zeros_like(h_ref)

    r4 = lax.broadcasted_iota(jnp.int32, (RW_G, RW_G), 0) // HD_C
    c4 = lax.broadcasted_iota(jnp.int32, (RW_G, RW_G), 1) // HD_C
    dmask = r4 == c4
    n_chunk = TM // RW_L
    links = [(j, slice(gi * RW_G, (gi + 1) * RW_G)) for j in range(n_chain)
             for gi in range(W_GRP // RW_G)]
    h = [h_ref[j, :, cols] for j, cols in links]
    for s in range(n_chunk):
        prods = []
        for (j, cols), hj in zip(links, h):
            g_ref, _, rp_ref, _ = ins[4 * j:4 * j + 4]
            c = n_chunk - 1 - s if j % 2 else s
            rows = slice(c * RW_L, (c + 1) * RW_L)
            h4 = jnp.where(dmask, jnp.concatenate([hj] * nl, axis=0), 0.0)
            hh, hl = _split_bf16(h4)
            gh, gl = _split_bf16(g_ref[0, 0, rows, cols])
            lh = jnp.concatenate([gh, rp_ref[0, 0, rows, cols]], axis=0)
            prods.append((_mm(lh, hh) + _mm(lh, hl), _mm(gl, hh)))
        for k, ((j, cols), (prod, corr)) in enumerate(zip(links, prods)):
            _, q_ref, _, yp_ref = ins[4 * j:4 * j + 4]
            c = n_chunk - 1 - s if j % 2 else s
            rows = slice(c * RW_L, (c + 1) * RW_L)
            y_refs[j][0, 0, rows, cols] = (prod[RW_L:] + yp_ref[0, 0, rows, cols].astype(F32)).astype(BF16)
            h[k] = prod[:RW_L] + corr + q_ref[0, 0, rows, cols]
    for (j, cols), hj in zip(links, h):
        h_ref[j, :, cols] = hj


def _scan_block(dr, i, n_blk):
    return jnp.where(dr == 0, i, jnp.where(i == 0, 0, n_blk - i))


def _rwkv(pc, p):
    b, t, _ = pc.shape
    nblk = t // TM
    hb = TM // 16
    lblk = 4 * W_GRP // LORA_PAD

    def halo_idx(dr, i):
        return jnp.where(dr == 0, jnp.maximum(i * hb - 1, 0), jnp.minimum((i + 1) * hb, nblk * hb - 1))

    per_dir = lambda n: pl.BlockSpec((1, 1, n), lambda bb, dr, i: (dr, 0, 0))
    per_dir_m = lambda m, n: pl.BlockSpec((1, m, n), lambda bb, dr, i: (dr, 0, 0))
    state = lambda: pl.BlockSpec((1, 1, TM, W_GRP), lambda bb, dr, i: (bb, dr, i, 0))
    sds = lambda dt: jax.ShapeDtypeStruct((b, 2, t, W_GRP), dt)
    gm, qm, rp, yp, bonus = pl.pallas_call(
        functools.partial(_rwkv_prep_kernel, nblk),
        grid=(b, 2, nblk),
        in_specs=[pl.BlockSpec((1, TM, 3 * W_GRP), lambda bb, dr, i: (bb, i, 0)),
                  pl.BlockSpec((1, TM, LORA_PAD), lambda bb, dr, i: (bb, i, lblk + dr)),
                  pl.BlockSpec((1, 16, 3 * W_GRP), lambda bb, dr, i: (bb, halo_idx(dr, i), 0)),
                  pl.BlockSpec((1, 16, LORA_PAD), lambda bb, dr, i: (bb, halo_idx(dr, i), lblk + dr)),
                  per_dir(3 * W_GRP), per_dir(LORA_PAD), per_dir(W_GRP), per_dir_m(LORA_PAD, W_GRP),
                  per_dir(W_GRP), per_dir_m(LORA_PAD, W_GRP), per_dir(W_GRP), per_dir(W_GRP),
                  per_dir(W_GRP)],
        out_specs=[state()] * 5,
        out_shape=[sds(F32), sds(F32), sds(BF16), sds(BF16), sds(BF16)],
        compiler_params=_params(("parallel", "parallel", "parallel")),
        name="rwkv_prep",
    )(pc, pc, pc, pc, p["mu_rkv"], p["mu_lora"], p["w0"], p["w2"], p["a0"], p["a2"],
      p["kk"], p["ka"], p["rk"])

    chains = [(bb, dr) for bb in range(b) for dr in range(2)]
    scan_spec = lambda bb, dr: pl.BlockSpec(
        (1, 1, TM, W_GRP), lambda i: (bb, dr, _scan_block(dr, i, nblk), 0))
    sds1 = jax.ShapeDtypeStruct((1, 1, t, W_GRP), BF16)
    ys = pl.pallas_call(
        functools.partial(_rwkv_scan_kernel, len(chains)),
        grid=(nblk,),
        in_specs=[scan_spec(bb, dr) for bb, dr in chains for _ in range(4)],
        out_specs=[pl.BlockSpec((1, 1, TM, W_GRP), lambda i, dr=dr: (0, 0, _scan_block(dr, i, nblk), 0))
                   for _, dr in chains],
        out_shape=[sds1] * len(chains),
        scratch_shapes=[pltpu.VMEM((len(chains), RW_L, W_GRP), F32)],
        compiler_params=_params(("arbitrary",)),
        name="rwkv_scan",
    )(*[a for _ in chains for a in (gm, qm, rp, yp)])
    return ys, bonus


def _outproj_kernel(mod_rows, *refs):
    n_batch = len(mod_rows)
    oa_ref, ob_ref, od_ref = refs[:3]
    y_refs = refs[3:3 + 2 * n_batch]
    bonus_ref, gc_ref, lw_ref, lb_ref, w_ref, x_ref, mod_ref, g_ref, o_ref = refs[3 + 2 * n_batch:]
    d = x_ref.shape[-1]
    ri = lax.broadcasted_iota(jnp.int32, (W_GRP, W_GRP), 0) // HD_C
    ci = lax.broadcasted_iota(jnp.int32, (W_GRP, W_GRP), 1) // HD_C
    ones_bd = jnp.where(ri == ci, 1.0, 0.0).astype(BF16)
    for bb in range(n_batch):
        yc = y_refs[2 * bb][0, 0].astype(F32) + y_refs[2 * bb + 1][0, 0].astype(F32)
        yc = yc - _mm(yc.astype(BF16), ones_bd) * (1.0 / HD_C)
        var = _mm((yc * yc).astype(BF16), ones_bd) * (1.0 / HD_C)
        yn = yc * lax.rsqrt(var + GN_EPS) * lw_ref[...] + lb_ref[...]
        oc = yn + bonus_ref[bb, 0].astype(F32) + bonus_ref[bb, 1].astype(F32)
        oc = (oc * _silu(gc_ref[bb].astype(F32))).astype(BF16)
        mix = jnp.concatenate([oa_ref[bb], ob_ref[bb], oc, od_ref[bb]], axis=1)
        acc = _mm(mix, w_ref[0])
        y = acc * lax.rsqrt(jnp.mean(acc * acc, axis=-1, keepdims=True) + EPS) * g_ref[...]
        o_ref[bb] = x_ref[bb] + mod_ref[0, mod_rows[bb]:mod_rows[bb] + 1, 2 * d:] * y


def _outproj(o_a, o_b, o_d, ys, bonus, pc, ln_w, ln_b, w_out, x, mods, layer, g_post, blk0, ob_blk0,
             mod_rows):
    b, n, d = x.shape
    tok = lambda off, col: pl.BlockSpec((b, TM, W_GRP), lambda i: (0, i + off, col))
    return pl.pallas_call(
        functools.partial(_outproj_kernel, mod_rows),
        grid=(n // TM,),
        in_specs=[tok(blk0, 0), tok(ob_blk0, 0), tok(blk0, 0)]
        + [pl.BlockSpec((1, 1, TM, W_GRP), lambda i: (0, 0, i + blk0, 0))] * len(ys)
        + [pl.BlockSpec((b, 2, TM, W_GRP), lambda i: (0, 0, i + blk0, 0)),
           tok(blk0, 3),
           pl.BlockSpec((1, W_GRP), lambda i: (0, 0)),
           pl.BlockSpec((1, W_GRP), lambda i: (0, 0)),
           pl.BlockSpec((1,) + w_out.shape[1:], lambda i: (layer, 0, 0)),
           pl.BlockSpec((b, TM, d), lambda i: (0, i, 0)),
           pl.BlockSpec((1, 8, 3 * d), lambda i: (layer, 0, 0)),
           pl.BlockSpec((1, d), lambda i: (0, 0))],
        out_specs=pl.BlockSpec((b, TM, d), lambda i: (0, i, 0)),
        out_shape=jax.ShapeDtypeStruct((b, n, d), F32),
        compiler_params=_params(("parallel",)),
        name="outproj",
    )(o_a, o_b, o_d, *ys, bonus, pc, ln_w, ln_b, w_out, x, mods, g_post.reshape(1, d))


def _w_in_segments():
    g = W_GRP
    o = [0]
    for s in (g, g, g, g, g, g, g, 3 * g, LORA_W + LORA_A, LORA_W + LORA_A, g, g, g, g, g):
        o.append(o[-1] + s)
    order = [0, 1, 2, 3, 4, 5, 6, 12, 13, 11, 14, 7, 10, 8, 9]
    segs, dst = [], 0
    for j in order:
        width = o[j + 1] - o[j]
        segs.append((o[j], width, dst))
        dst += -(-width // LORA_PAD) * LORA_PAD
    assert dst == N_ALL
    return segs, o[-1]


def _w_in_kernel(wt_ref, o_ref):
    segs, _ = _w_in_segments()
    cols = wt_ref.shape[2]
    for src, width, dst in segs:
        piece = wt_ref[0, src:src + width, :]
        padded = -(-width // LORA_PAD) * LORA_PAD
        if padded != width:
            piece = jnp.concatenate([piece, jnp.zeros((padded - width, cols), F32)], axis=0)
        o_ref[0, :, dst:dst + padded] = piece.T.astype(BF16)


def _regroup_w_in(w_in):
    depth, d, n_in = w_in.shape
    assert n_in == _w_in_segments()[1]
    tr = 256
    return pl.pallas_call(
        _w_in_kernel,
        grid=(depth, d // tr),
        in_specs=[pl.BlockSpec((1, n_in, tr), lambda l, i: (l, 0, i))],
        out_specs=pl.BlockSpec((1, tr, N_ALL), lambda l, i: (l, i, 0)),
        out_shape=jax.ShapeDtypeStruct((depth, d, N_ALL), BF16),
        compiler_params=_params(("parallel", "parallel")),
        name="w_in_prep",
    )(jnp.swapaxes(w_in, 1, 2))


def _cast_kernel(w_ref, o_ref):
    o_ref[...] = w_ref[...].astype(BF16)


def _cast_w_out(w_out):
    depth, k, d = w_out.shape
    tr = 512
    return pl.pallas_call(
        _cast_kernel,
        grid=(depth, k // tr),
        in_specs=[pl.BlockSpec((1, tr, d), lambda l, i: (l, i, 0))],
        out_specs=pl.BlockSpec((1, tr, d), lambda l, i: (l, i, 0)),
        out_shape=jax.ShapeDtypeStruct((depth, k, d), BF16),
        compiler_params=_params(("parallel", "parallel")),
        name="w_out_prep",
    )(w_out)


def _rope_tables(n_ctx, n_lat):
    n_rows = n_lat // GRID_W
    inv = ROPE_BASE ** (-jnp.arange(0, 32, 2, dtype=F32) / 32)
    lane = jnp.arange(HD_B)
    use_col = ((lane // 32) % 2 == 1)[None, None, :]
    trig = lambda f, n: f(jnp.arange(n, dtype=F32)[:, None] * inv[lane % 16][None, :])
    grid = lambda f: jnp.where(use_col, trig(f, GRID_W)[None, :, :], trig(f, n_rows)[:, None, :]
                               ).reshape(n_lat, HD_B)
    first = (lane % 32 < 16)[None, :]
    cos, sin = grid(jnp.cos), grid(jnp.sin)
    sa = jnp.where(first, -sin, 0.0)
    sb = jnp.where(first, 0.0, sin)
    ident = lambda v: jnp.full((n_ctx, HD_B), v, F32)
    return (jnp.concatenate([ident(1.0), cos]), jnp.concatenate([ident(0.0), sa]),
            jnp.concatenate([ident(0.0), sb]))


def _rwkv_params(l, mu, w0, w2, a0, a2, kk, ka, rk, ln_w, ln_b):
    g = W_GRP
    zpad = lambda n: jnp.zeros((2, n), F32)
    mu_l = jnp.concatenate([mu[l][:, 3 * g:], zpad(LORA_PAD - LORA_W - LORA_A)], axis=1)
    w2p = jnp.concatenate([w2[l], jnp.zeros((2, LORA_PAD - LORA_W, g), F32)], axis=1)
    a2p = jnp.concatenate([jnp.zeros((2, LORA_W, g), F32), a2[l],
                           jnp.zeros((2, LORA_PAD - LORA_W - LORA_A, g), F32)], axis=1)
    r3 = lambda v: v.reshape(2, 1, -1)
    return dict(mu_rkv=r3(mu[l][:, :3 * g]), mu_lora=r3(mu_l), w0=r3(w0[l]), w2=w2p, a0=r3(a0[l]),
                a2=a2p, kk=r3(kk[l]), ka=r3(ka[l]), rk=r3(rk[l].reshape(2, g)),
                ln_w=ln_w[l].reshape(1, g), ln_b=ln_b[l].reshape(1, g))


def kernel(x, c, ctx, c_ctx, w_mod, b_mod, g_pre, g_post, w_in, w_out, sgu_w, sgu_b, lam_q1, lam_k1, lam_q2, lam_k2, subln_g, rwkv_mu, rwkv_w0, rwkv_w2, rwkv_a0, rwkv_a2, rwkv_kk, rwkv_ka, rwkv_rk, rwkv_ln_w, rwkv_ln_b, conv_w):
    depth = w_mod.shape[0]
    b, n_lat, _ = x.shape
    n_ctx = ctx.shape[1]
    mods = _modulation(c, c_ctx, w_mod, b_mod)
    rope = _rope_tables(n_ctx, n_lat)
    w_in_b = _regroup_w_in(w_in)
    w_out_b = _cast_w_out(w_out)
    xc = ctx
    for l in range(depth):
        need_ctx = l < depth - 1
        lam_init = 0.8 - 0.6 * math.exp(-0.3 * l)
        pa, pb, pd, pc = _inproj(xc, x, mods, l, g_pre[l], w_in_b, rope)
        o_a, o_d = _local_mixers(pa, pd, sgu_w[l], sgu_b[l], conv_w[l])
        lam_params = jnp.stack([lam_q1[l], lam_k1[l], lam_q2[l], lam_k2[l]])
        cb = n_ctx // TM
        o_b = _attention(pb, lam_params, subln_g[l], lam_init, cb, n_lat, n_ctx + n_lat, ATT_NSUB)
        rp = _rwkv_params(l, rwkv_mu, rwkv_w0, rwkv_w2, rwkv_a0, rwkv_a2, rwkv_kk, rwkv_ka, rwkv_rk,
                          rwkv_ln_w, rwkv_ln_b)
        ys, bonus = _rwkv(pc, rp)
        tail = (ys, bonus, pc, rp["ln_w"], rp["ln_b"], w_out_b)
        x_new = _outproj(o_a, o_b, o_d, *tail, x, mods, l, g_post[l], cb, 0, tuple(range(b)))
        if need_ctx:
            o_bc = _attention(pb, lam_params, subln_g[l], lam_init, 0, n_ctx, n_ctx, 1)
            xc = _outproj(o_a, o_bc, o_d, *tail, xc, mods, l, g_post[l], 0, 0, (b,) * b)
        x = x_new
    return x
```

```python
import functools
import math

import jax
import jax.numpy as jnp
from jax import lax
from jax.experimental import pallas as pl
from jax.experimental.pallas import tpu as pltpu

F32 = jnp.float32
BF16 = jnp.bfloat16

EPS = 1e-6
GN_EPS = 64e-5
GRID_W = 64
ROPE_BASE = 10000.0
LOG2E = 1.4426950408889634

W_GRP = 512
CHUNK_A = 128
HD_B = 128
HD_QK = 64
HD_C = 64
LORA_W = 64
LORA_A = 32
LORA_PAD = 128
RW_L = 64
RW_G = 256
TM = 256
ATT_KB = 512
ATT_NSUB = 4
ATT_CHAIN = 2
VMEM_LIMIT = 56 * 1024 * 1024

N_A = 3 * W_GRP
N_B = 4 * W_GRP
N_D = 4 * W_GRP
N_C = 4 * W_GRP + 2 * LORA_PAD
OFF_A, OFF_B, OFF_D, OFF_C = 0, N_A, N_A + N_B, N_A + N_B + N_D
N_ALL = OFF_C + N_C


def _silu(g):
    return g * (1.0 / (1.0 + jnp.exp(-g)))


def _sigmoid(x):
    return 1.0 / (1.0 + jnp.exp(-x))


def _split_bf16(x):
    hi = x.astype(BF16)
    lo = (x - hi.astype(F32)).astype(BF16)
    return hi, lo


def _mm(a, b):
    return jnp.dot(a, b, preferred_element_type=F32)


def _mm_nt(a, b):
    return lax.dot_general(a, b, (((1,), (1,)), ((), ())), preferred_element_type=F32)


def _params(sem):
    return pltpu.CompilerParams(dimension_semantics=sem, vmem_limit_bytes=VMEM_LIMIT)


def _mod_kernel(s_ref, w_ref, b_ref, o_ref):
    s = _silu(s_ref[...])
    o_ref[0] = _mm(s.astype(BF16), w_ref[0].astype(BF16)) + b_ref[0]


def _modulation(c, c_ctx, w_mod, b_mod):
    depth, d, n3 = w_mod.shape
    b = c.shape[0]
    rows = jnp.zeros((8, d), F32).at[:b].set(c).at[b].set(c_ctx)
    tn = 768
    return pl.pallas_call(
        _mod_kernel,
        grid=(depth, n3 // tn),
        in_specs=[pl.BlockSpec((8, d), lambda l, j: (0, 0)),
                  pl.BlockSpec((1, d, tn), lambda l, j: (l, 0, j)),
                  pl.BlockSpec((1, 1, tn), lambda l, j: (l, 0, j))],
        out_specs=pl.BlockSpec((1, 8, tn), lambda l, j: (l, 0, j)),
        out_shape=jax.ShapeDtypeStruct((depth, 8, n3), F32),
        compiler_params=_params(("parallel", "parallel")),
        name="modulation",
    )(rows, w_mod, b_mod.reshape(depth, 1, n3))


def _inproj_kernel(n_batch, xc_ref, xl_ref, mod_ref, g_ref, w_ref, cos_ref, sa_ref, sb_ref,
                   pa_ref, pb_ref, pd_ref, pc_ref):
    b = pl.program_id(0)
    i = pl.program_id(1)
    d = xc_ref.shape[-1]
    is_ctx = i == 0
    x = jnp.where(is_ctx, xc_ref[0], xl_ref[0])
    y = x * lax.rsqrt(jnp.mean(x * x, axis=-1, keepdims=True) + EPS) * g_ref[...]
    row = jnp.where(is_ctx, mod_ref[0, pl.ds(n_batch, 1), :], mod_ref[0, pl.ds(b, 1), :])
    h = (y * (1.0 + row[:, d:2 * d]) + row[:, :d]).astype(BF16)

    def proj(c0, c1):
        return _mm(h, w_ref[0, :, c0:c1])

    pa_ref[0] = proj(OFF_A, OFF_A + N_A).astype(BF16)
    cos, sa, sb = cos_ref[...], sa_ref[...], sb_ref[...]
    qk = proj(OFF_B, OFF_B + 2 * W_GRP)
    for j in range(2 * W_GRP // HD_B):
        t = qk[:, j * HD_B:(j + 1) * HD_B]
        t = t * cos + pltpu.roll(t, HD_B - 16, axis=1) * sa + pltpu.roll(t, 16, axis=1) * sb
        if j < W_GRP // HD_B:
            t = t * (HD_QK ** -0.5 * LOG2E)
        pb_ref[0, :, j * HD_B:(j + 1) * HD_B] = t.astype(BF16)
    pb_ref[0, :, 2 * W_GRP:] = proj(OFF_B + 2 * W_GRP, OFF_B + N_B).astype(BF16)
    pd_ref[0] = proj(OFF_D, OFF_D + N_D).astype(BF16)
    pc_ref[0] = proj(OFF_C, OFF_C + N_C).astype(BF16)


def _inproj(xc, xl, mods, layer, g_pre, w_in, rope):
    b, n_ctx, d = xc.shape
    n_lat = xl.shape[1]
    t = n_ctx + n_lat
    assert n_ctx == TM and n_lat % TM == 0
    nblk = t // TM
    cos, sa, sb = rope
    tab = pl.BlockSpec((TM, HD_B), lambda bb, i: (i, 0))
    out = lambda n: pl.BlockSpec((1, TM, n), lambda bb, i: (bb, i, 0))
    return pl.pallas_call(
        functools.partial(_inproj_kernel, b),
        grid=(b, nblk),
        in_specs=[pl.BlockSpec((1, TM, d), lambda bb, i: (bb, 0, 0)),
                  pl.BlockSpec((1, TM, d), lambda bb, i: (bb, jnp.maximum(i - 1, 0), 0)),
                  pl.BlockSpec((1, 8, 3 * d), lambda bb, i: (layer, 0, 0)),
                  pl.BlockSpec((1, d), lambda bb, i: (0, 0)),
                  pl.BlockSpec((1, d, N_ALL), lambda bb, i: (layer, 0, 0), pipeline_mode=pl.Buffered(1)),
                  tab, tab, tab],
        out_specs=[out(N_A), out(N_B), out(N_D), out(N_C)],
        out_shape=[jax.ShapeDtypeStruct((b, t, n), BF16) for n in (N_A, N_B, N_D, N_C)],
        compiler_params=_params(("parallel", "arbitrary")),
        name="inproj",
    )(xc, xl, mods, g_pre.reshape(1, d), w_in, cos, sa, sb)


def _attn_kernel(lam_init, nsub, *refs):
    q_refs, (k_ref, v_ref), g_refs = refs[:nsub], refs[nsub:nsub + 2], refs[nsub + 2:2 * nsub + 2]
    lam_ref, sg_ref, o_ref, vext_ref = refs[2 * nsub + 2:]
    tq = nsub * TM
    n_keys = k_ref.shape[1]

    @pl.when(pl.program_id(2) == 0)
    def _():
        vext_ref[:, :HD_B] = v_ref[0]
        vext_ref[:, HD_B:] = jnp.ones((n_keys, HD_B), BF16)

    lp = lam_ref[...]
    lam = (jnp.exp(jnp.sum(lp[0:1] * lp[1:2], axis=-1, keepdims=True))
           - jnp.exp(jnp.sum(lp[2:3] * lp[3:4], axis=-1, keepdims=True)) + lam_init)
    lane = lax.broadcasted_iota(jnp.int32, (1, HD_B), 1)
    n_chain = -(-nsub // ATT_CHAIN)
    tc = tq // n_chain
    q2s = []
    for ci in range(n_chain):
        q = jnp.concatenate([r[0] for r in q_refs[ci * ATT_CHAIN:(ci + 1) * ATT_CHAIN]], axis=0)
        zero = jnp.zeros_like(q)
        q2s.append(jnp.concatenate([jnp.where(lane < HD_QK, q, zero),
                                    jnp.where(lane >= HD_QK, q, zero)], axis=0))
    ms = [None] * n_chain
    os_ = [None] * n_chain
    for k0 in range(0, n_keys, ATT_KB):
        k1 = min(k0 + ATT_KB, n_keys)
        for ci in range(n_chain):
            s = _mm_nt(q2s[ci], k_ref[0, k0:k1])
            bm = jnp.max(s, axis=-1, keepdims=True)
            if ms[ci] is None:
                ms[ci] = bm
                os_[ci] = _mm(jnp.exp2(s - bm).astype(BF16), vext_ref[k0:k1])
            else:
                m_new = jnp.maximum(ms[ci], bm)
                os_[ci] = (os_[ci] * jnp.exp2(ms[ci] - m_new)
                           + _mm(jnp.exp2(s - m_new).astype(BF16), vext_ref[k0:k1]))
                ms[ci] = m_new
    for ci in range(n_chain):
        o = os_[ci]
        a = o[:tc, :HD_B] / o[:tc, HD_B:] - lam * (o[tc:, :HD_B] / o[tc:, HD_B:])
        y = a * lax.rsqrt(jnp.mean(a * a, axis=-1, keepdims=True) + EPS) * sg_ref[...] * (1.0 - lam_init)
        g = jnp.concatenate([r[0] for r in g_refs[ci * ATT_CHAIN:(ci + 1) * ATT_CHAIN]], axis=0)
        o_ref[0, ci * tc:(ci + 1) * tc, :] = (y * _silu(g.astype(F32))).astype(BF16)


def _attention(pb, lam_params, subln_g, lam_init, q_blk0, n_q, n_keys, nsub):
    b = pb.shape[0]
    nh = W_GRP // HD_B
    assert n_q % (nsub * TM) == 0
    q_spec = lambda col0, j: pl.BlockSpec(
        (1, TM, HD_B), lambda bb, h, i: (bb, nsub * i + q_blk0 + j, col0 + h))
    kv_spec = lambda col0: pl.BlockSpec((1, n_keys, HD_B), lambda bb, h, i: (bb, 0, col0 + h))
    return pl.pallas_call(
        functools.partial(_attn_kernel, lam_init, nsub),
        grid=(b, nh, n_q // (nsub * TM)),
        in_specs=([q_spec(0, j) for j in range(nsub)] + [kv_spec(nh), kv_spec(2 * nh)]
                  + [q_spec(3 * nh, j) for j in range(nsub)]
                  + [pl.BlockSpec((4, HD_QK), lambda bb, h, i: (0, 0)),
                     pl.BlockSpec((1, HD_B), lambda bb, h, i: (0, 0))]),
        out_specs=pl.BlockSpec((1, nsub * TM, HD_B), lambda bb, h, i: (bb, i, h)),
        out_shape=jax.ShapeDtypeStruct((b, n_q, W_GRP), BF16),
        scratch_shapes=[pltpu.VMEM((n_keys, 2 * HD_B), BF16)],
        compiler_params=_params(("parallel", "parallel", "arbitrary")),
        name="diff_attention",
    )(*([pb] * (2 * nsub + 2)), lam_params, subln_g.reshape(1, HD_B))


def _seg_sum(x, ones_bd):
    hi, lo = _split_bf16(x)
    return _mm(hi, ones_bd) + _mm(lo, ones_bd)


def _head_diag(x4, mask):
    return jnp.where(mask, x4, 0.0).astype(BF16)


def _rwkv_prep_kernel(n_blk, rkv_ref, lo_ref, hrkv_ref, hlo_ref, mu_ref, mul_ref, w0_ref, w2_ref,
                      a0_ref, a2_ref, kk_ref, ka_ref, rk_ref,
                      g_ref, q_ref, rp_ref, yp_ref, bonus_ref):
    dr = pl.program_id(1)
    i = pl.program_id(2)
    fwd = dr == 0
    nl = RW_G // HD_C
    sgn = 1 - 2 * dr
    hrow = jnp.where(fwd, 15, 0)
    edge = jnp.logical_or(i == 0, i == jnp.where(fwd, 1, n_blk - 1))
    rsel = lax.broadcasted_iota(jnp.int32, (TM, 1), 0) == jnp.where(fwd, 0, TM - 1)
    hsel = lax.broadcasted_iota(jnp.int32, (16, 1), 0) == hrow

    def shifted(cur_ref, halo_ref, mu):
        f = cur_ref[0].astype(F32)
        halo = jnp.sum(jnp.where(hsel, halo_ref[0].astype(F32), 0.0), axis=0, keepdims=True)
        halo = jnp.where(edge, 0.0, halo)
        rolled = jnp.where(fwd, pltpu.roll(f, 1, axis=0), pltpu.roll(f, TM - 1, axis=0))
        prev = jnp.where(rsel, halo, rolled)
        return f + (prev - f) * mu

    z = shifted(rkv_ref, hrkv_ref, mu_ref[0])
    zl = shifted(lo_ref, hlo_ref, mul_ref[0])
    r, k, v = z[:, :W_GRP], z[:, W_GRP:2 * W_GRP], z[:, 2 * W_GRP:]
    wl = w0_ref[0] + _mm(jnp.tanh(zl).astype(BF16), w2_ref[0].astype(BF16))
    logd = -math.exp(-0.5) * _sigmoid(wl)
    a = _sigmoid(a0_ref[0] + _mm(zl.astype(BF16), a2_ref[0].astype(BF16)))
    ri = lax.broadcasted_iota(jnp.int32, (W_GRP, W_GRP), 0) // HD_C
    ci = lax.broadcasted_iota(jnp.int32, (W_GRP, W_GRP), 1) // HD_C
    ones_bd = jnp.where(ri == ci, 1.0, 0.0).astype(BF16)
    kk = k * kk_ref[0]
    kk = kk / jnp.maximum(jnp.sqrt(_seg_sum(kk * kk, ones_bd)), 1e-12)
    k2 = k * (1.0 + (a - 1.0) * ka_ref[0])
    bv = kk * a
    bonus_ref[0, 0] = (_seg_sum(r * k2 * rk_ref[0], ones_bd) * v).astype(BF16)
    rt = lax.broadcasted_iota(jnp.int32, (TM, TM), 0)
    ct = lax.broadcasted_iota(jnp.int32, (TM, TM), 1)
    same = rt // RW_L == ct // RW_L
    tri = jnp.where(jnp.logical_and(same, (rt - ct) * sgn >= 0), 1.0, 0.0).astype(BF16)
    blk = jnp.where(same, 1.0, 0.0).astype(BF16)
    lh, ll = _split_bf16(logd)
    cum = _mm(tri, lh) + _mm(tri, ll)
    tot = _mm(blk, lh) + _mm(blk, ll)
    half = 0.5 * tot
    e_pos = jnp.exp(cum - half)
    e_neg = jnp.exp(half - cum)
    e_prev = jnp.exp(cum - logd - half)
    e_end = jnp.exp(tot - cum)
    g_end = jnp.exp(tot)
    e_half = jnp.exp(half)
    kt_all = kk * e_prev
    rt_all = r * e_pos
    kh_all = k2 * e_neg
    bh_all = bv * e_neg
    kb_all = k2 * e_end
    bb_all = bv * e_end
    row = lax.broadcasted_iota(jnp.int32, (RW_L, RW_G), 0)
    col = lax.broadcasted_iota(jnp.int32, (RW_L, RW_G), 1) % HD_C
    eye = jnp.where(col == row, 1.0, 0.0)
    r4 = lax.broadcasted_iota(jnp.int32, (RW_G, RW_G), 0) // HD_C
    c4 = lax.broadcasted_iota(jnp.int32, (RW_G, RW_G), 1) // HD_C
    dmask = r4 == c4
    r2 = lax.broadcasted_iota(jnp.int32, (2 * RW_L, RW_G), 0) % RW_L
    c2 = lax.broadcasted_iota(jnp.int32, (2 * RW_L, RW_G), 1) % HD_C
    first = jnp.where(lax.broadcasted_iota(jnp.int32, (2 * RW_L, RW_G), 0) < RW_L, 1, 0)
    mask2 = (r2 - c2) * sgn >= first

    def bd(x):
        return _head_diag(jnp.concatenate([x] * nl, axis=0), dmask)

    def hprod(x, y):
        return _mm(x.astype(BF16), bd(y))

    def diag_blocks(full):
        out = jnp.zeros((RW_L, RW_G), F32)
        lane_h = lax.broadcasted_iota(jnp.int32, (RW_L, RW_G), 1) // HD_C
        for hh in range(nl):
            out = out + jnp.where(lane_h == hh, full[hh * HD_C:(hh + 1) * HD_C, :], 0.0)
        return out

    pairs = [(slice(c * RW_L, (c + 1) * RW_L), slice(gi * RW_G, (gi + 1) * RW_G))
             for c in range(TM // RW_L) for gi in range(W_GRP // RW_G)]
    cat = lambda *xs: jnp.concatenate(xs, axis=0)
    L = RW_L
    kt = [kt_all[p] for p in pairs]
    rtl = [rt_all[p] for p in pairs]
    vv = [v[p] for p in pairs]
    lhs = [cat(a_, b_).astype(BF16) for a_, b_ in zip(kt, rtl)]
    pb_ = [jnp.where(mask2, _mm_nt(x, bd(bh_all[p])), 0.0) for x, p in zip(lhs, pairs)]
    pk_ = [jnp.where(mask2, _mm_nt(x, bd(kh_all[p])), 0.0) for x, p in zip(lhs, pairs)]
    res = [hprod(x, x[:L]) for x in pb_]
    tinv = [eye - x[:L] for x in pb_]
    mt = [x[L:] - y[L:] for x, y in zip(pb_, res)]
    pw = [y[:L] for y in res]
    for _ in range(4):
        res = [hprod(cat(t_, m_, w_), w_) for t_, m_, w_ in zip(tinv, mt, pw)]
        tinv = [t_ + y[:L] for t_, y in zip(tinv, res)]
        mt = [m_ + y[L:2 * L] for m_, y in zip(mt, res)]
        pw = [y[2 * L:] for y in res]
    res = [hprod(cat(t_, m_), w_) for t_, m_, w_ in zip(tinv, mt, pw)]
    tm = [cat(t_, m_) + y for t_, m_, y in zip(tinv, mt, res)]
    wv = [hprod(x, y) for x, y in zip(pk_, vv)]
    tmb = [x.astype(BF16) for x in tm]
    ww = [_mm(x, bd(y)) for x, y in zip(tmb, kt)]
    uu = [_mm(x, bd(y[:L])) for x, y in zip(tmb, wv)]
    for j, p in enumerate(pairs):
        rows, cols = p
        eh = e_half[rows.start:rows.start + 1, cols]
        rp_ref[0, 0, rows, cols] = ((rtl[j] - ww[j][L:]) * eh).astype(BF16)
        yp_ref[0, 0, rows, cols] = (wv[j][L:] - uu[j][L:]).astype(BF16)
        xt = cat(kb_all[p], -bb_all[p]).T.astype(BF16)
        zeros = jnp.zeros((L, RW_G), F32)
        yq = cat(vv[j], uu[j][:L]).astype(BF16)
        yg = cat(zeros, ww[j][:L]).astype(BF16)
        q_ref[0, 0, rows, cols] = diag_blocks(_mm(xt, yq))
        g_ref[0, 0, rows, cols] = (diag_blocks(_mm(xt, yg)) * eh
                                   + eye * g_end[rows.start:rows.start + 1, cols])


def _rwkv_scan_kernel(n_chain, *refs):
    ins, y_refs, h_ref = refs[:4 * n_chain], refs[4 * n_chain:5 * n_chain], refs[-1]
    nl = RW_G // HD_C

    @pl.when(pl.program_id(0) == 0)
    def _():
        h_ref[...] = jnp.zeros_like(h_ref)

    r4 = lax.broadcasted_iota(jnp.int32, (RW_G, RW_G), 0) // HD_C
    c4 = lax.broadcasted_iota(jnp.int32, (RW_G, RW_G), 1) // HD_C
    dmask = r4 == c4
    n_chunk = TM // RW_L
    links = [(j, slice(gi * RW_G, (gi + 1) * RW_G)) for j in range(n_chain)
             for gi in range(W_GRP // RW_G)]
    h = [h_ref[j, :, cols] for j, cols in links]
    for s in range(n_chunk):
        prods = []
        for (j, cols), hj in zip(links, h):
            g_ref, _, rp_ref, _ = ins[4 * j:4 * j + 4]
            c = n_chunk - 1 - s if j % 2 else s
            rows = slice(c * RW_L, (c + 1) * RW_L)
            h4 = jnp.where(dmask, jnp.concatenate([hj] * nl, axis=0), 0.0)
            hh, hl = _split_bf16(h4)
            gh, gl = _split_bf16(g_ref[0, 0, rows, cols])
            lh = jnp.concatenate([gh, rp_ref[0, 0, rows, cols]], axis=0)
            prods.append((_mm(lh, hh) + _mm(lh, hl), _mm(gl, hh)))
        for k, ((j, cols), (prod, corr)) in enumerate(zip(links, prods)):
            _, q_ref, _, yp_ref = ins[4 * j:4 * j + 4]
            c = n_chunk - 1 - s if j % 2 else s
            rows = slice(c * RW_L, (c + 1) * RW_L)
            y_refs[j][0, 0, rows, cols] = (prod[RW_L:] + yp_ref[0, 0, rows, cols].astype(F32)).astype(BF16)
            h[k] = prod[:RW_L] + corr + q_ref[0, 0, rows, cols]
    for (j, cols), hj in zip(links, h):
        h_ref[j, :, cols] = hj


def _scan_block(dr, i, n_blk):
    return jnp.where(dr == 0, i, jnp.where(i == 0, 0, n_blk - i))


def _rwkv(pc, p):
    b, t, _ = pc.shape
    nblk = t // TM
    hb = TM // 16
    lblk = 4 * W_GRP // LORA_PAD

    def halo_idx(dr, i):
        return jnp.where(dr == 0, jnp.maximum(i * hb - 1, 0), jnp.minimum((i + 1) * hb, nblk * hb - 1))

    per_dir = lambda n: pl.BlockSpec((1, 1, n), lambda bb, dr, i: (dr, 0, 0))
    per_dir_m = lambda m, n: pl.BlockSpec((1, m, n), lambda bb, dr, i: (dr, 0, 0))
    state = lambda: pl.BlockSpec((1, 1, TM, W_GRP), lambda bb, dr, i: (bb, dr, i, 0))
    sds = lambda dt: jax.ShapeDtypeStruct((b, 2, t, W_GRP), dt)
    gm, qm, rp, yp, bonus = pl.pallas_call(
        functools.partial(_rwkv_prep_kernel, nblk),
        grid=(b, 2, nblk),
        in_specs=[pl.BlockSpec((1, TM, 3 * W_GRP), lambda bb, dr, i: (bb, i, 0)),
                  pl.BlockSpec((1, TM, LORA_PAD), lambda bb, dr, i: (bb, i, lblk + dr)),
                  pl.BlockSpec((1, 16, 3 * W_GRP), lambda bb, dr, i: (bb, halo_idx(dr, i), 0)),
                  pl.BlockSpec((1, 16, LORA_PAD), lambda bb, dr, i: (bb, halo_idx(dr, i), lblk + dr)),
                  per_dir(3 * W_GRP), per_dir(LORA_PAD), per_dir(W_GRP), per_dir_m(LORA_PAD, W_GRP),
                  per_dir(W_GRP), per_dir_m(LORA_PAD, W_GRP), per_dir(W_GRP), per_dir(W_GRP),
                  per_dir(W_GRP)],
        out_specs=[state()] * 5,
        out_shape=[sds(F32), sds(F32), sds(BF16), sds(BF16), sds(BF16)],
        compiler_params=_params(("parallel", "parallel", "parallel")),
        name="rwkv_prep",
    )(pc, pc, pc, pc, p["mu_rkv"], p["mu_lora"], p["w0"], p["w2"], p["a0"], p["a2"],
      p["kk"], p["ka"], p["rk"])

    chains = [(bb, dr) for bb in range(b) for dr in range(2)]
    scan_spec = lambda bb, dr: pl.BlockSpec(
        (1, 1, TM, W_GRP), lambda i: (bb, dr, _scan_block(dr, i, nblk), 0))
    sds1 = jax.ShapeDtypeStruct((1, 1, t, W_GRP), BF16)
    ys = pl.pallas_call(
        functools.partial(_rwkv_scan_kernel, len(chains)),
        grid=(nblk,),
        in_specs=[scan_spec(bb, dr) for bb, dr in chains for _ in range(4)],
        out_specs=[pl.BlockSpec((1, 1, TM, W_GRP), lambda i, dr=dr: (0, 0, _scan_block(dr, i, nblk), 0))
                   for _, dr in chains],
        out_shape=[sds1] * len(chains),
        scratch_shapes=[pltpu.VMEM((len(chains), RW_L, W_GRP), F32)],
        compiler_params=_params(("arbitrary",)),
        name="rwkv_scan",
    )(*[a for _ in chains for a in (gm, qm, rp, yp)])
    return ys, bonus


def _sgu(pa_ref, bb, ws_ref, bs_ref):
    rows_out = []
    for c in range(TM // CHUNK_A):
        rows = slice(c * CHUNK_A, (c + 1) * CHUNK_A)
        heads = []
        for h in range(W_GRP // CHUNK_A):
            cols = lambda k: slice(k * W_GRP + h * CHUNK_A, k * W_GRP + (h + 1) * CHUNK_A)
            u = pa_ref[bb, rows, cols(0)].astype(F32)
            v = pa_ref[bb, rows, cols(1)].astype(F32)
            g = pa_ref[bb, rows, cols(2)].astype(F32)
            vc = v - jnp.mean(v, axis=-1, keepdims=True)
            vn = vc * lax.rsqrt(jnp.mean(vc * vc, axis=-1, keepdims=True) + EPS)
            mixed = _mm(ws_ref[h].astype(BF16), vn.astype(BF16)) + bs_ref[:, h:h + 1]
            heads.append((u * mixed * _silu(g)).astype(BF16))
        rows_out.append(jnp.concatenate(heads, axis=1))
    return jnp.concatenate(rows_out, axis=0)


def _short_conv(pd_ref, hp_ref, hn_ref, bb, cw_ref, first, last):
    cx = pd_ref[bb, :, 0:2 * W_GRP].astype(F32)
    z = cx[:, :W_GRP] * cx[:, W_GRP:]
    hp = hp_ref[bb].astype(F32)
    hn = hn_ref[bb].astype(F32)
    zp = jnp.where(first, 0.0, hp[15:16, :W_GRP] * hp[15:16, W_GRP:])
    zn = jnp.where(last, 0.0, hn[0:1, :W_GRP] * hn[0:1, W_GRP:])
    r = lax.broadcasted_iota(jnp.int32, (TM, 1), 0)
    z_prev = jnp.where(r == 0, zp, pltpu.roll(z, 1, axis=0))
    z_next = jnp.where(r == TM - 1, zn, pltpu.roll(z, TM - 1, axis=0))
    w = cw_ref[...]
    y = z_prev * w[0:1] + z * w[1:2] + z_next * w[2:3]
    bg = pd_ref[bb, :, 2 * W_GRP:3 * W_GRP].astype(F32)
    g = pd_ref[bb, :, 3 * W_GRP:].astype(F32)
    return (bg * y * _silu(g)).astype(BF16)


def _outproj_kernel(mod_rows, blk0, n_blk, *refs):
    n_batch = len(mod_rows)
    pa_ref, pd_ref, hp_ref, hn_ref, ws_ref, bs_ref, cw_ref, ob_ref = refs[:8]
    y_refs = refs[8:8 + 2 * n_batch]
    bonus_ref, gc_ref, lw_ref, lb_ref, w_ref, x_ref, mod_ref, g_ref, o_ref = refs[8 + 2 * n_batch:]
    d = x_ref.shape[-1]
    ib = pl.program_id(0) + blk0
    first = ib <= 1
    last = jnp.logical_or(ib == 0, ib == n_blk - 1)
    ri = lax.broadcasted_iota(jnp.int32, (W_GRP, W_GRP), 0) // HD_C
    ci = lax.broadcasted_iota(jnp.int32, (W_GRP, W_GRP), 1) // HD_C
    ones_bd = jnp.where(ri == ci, 1.0, 0.0).astype(BF16)
    for bb in range(n_batch):
        oa = _sgu(pa_ref, bb, ws_ref, bs_ref)
        od = _short_conv(pd_ref, hp_ref, hn_ref, bb, cw_ref, first, last)
        yc = y_refs[2 * bb][0, 0].astype(F32) + y_refs[2 * bb + 1][0, 0].astype(F32)
        yc = yc - _mm(yc.astype(BF16), ones_bd) * (1.0 / HD_C)
        var = _mm((yc * yc).astype(BF16), ones_bd) * (1.0 / HD_C)
        yn = yc * lax.rsqrt(var + GN_EPS) * lw_ref[...] + lb_ref[...]
        oc = yn + bonus_ref[bb, 0].astype(F32) + bonus_ref[bb, 1].astype(F32)
        oc = (oc * _silu(gc_ref[bb].astype(F32))).astype(BF16)
        mix = jnp.concatenate([oa, ob_ref[bb], oc, od], axis=1)
        acc = _mm(mix, w_ref[0])
        y = acc * lax.rsqrt(jnp.mean(acc * acc, axis=-1, keepdims=True) + EPS) * g_ref[...]
        o_ref[bb] = x_ref[bb] + mod_ref[0, mod_rows[bb]:mod_rows[bb] + 1, 2 * d:] * y


def _outproj(pa, pd, sgu_w, sgu_b, conv_w, o_b, ys, bonus, pc, ln_w, ln_b, w_out, x, mods, layer,
             g_post, blk0, ob_blk0, mod_rows):
    b, n, d = x.shape
    t = pa.shape[1]
    nblk = t // TM
    hb = TM // 16
    tok = lambda off, col, width: pl.BlockSpec((b, TM, width), lambda i: (0, i + off, col))
    return pl.pallas_call(
        functools.partial(_outproj_kernel, mod_rows, blk0, nblk),
        grid=(n // TM,),
        in_specs=[tok(blk0, 0, N_A), tok(blk0, 0, N_D),
                  pl.BlockSpec((b, 16, 2 * W_GRP),
                               lambda i: (0, jnp.maximum((i + blk0) * hb - 1, 0), 0)),
                  pl.BlockSpec((b, 16, 2 * W_GRP),
                               lambda i: (0, jnp.minimum((i + blk0 + 1) * hb, nblk * hb - 1), 0)),
                  pl.BlockSpec(sgu_w.shape, lambda i: (0, 0, 0)),
                  pl.BlockSpec((CHUNK_A, sgu_w.shape[0]), lambda i: (0, 0)),
                  pl.BlockSpec(conv_w.shape, lambda i: (0, 0)),
                  tok(ob_blk0, 0, W_GRP)]
        + [pl.BlockSpec((1, 1, TM, W_GRP), lambda i: (0, 0, i + blk0, 0))] * len(ys)
        + [pl.BlockSpec((b, 2, TM, W_GRP), lambda i: (0, 0, i + blk0, 0)),
           tok(blk0, 3, W_GRP),
           pl.BlockSpec((1, W_GRP), lambda i: (0, 0)),
           pl.BlockSpec((1, W_GRP), lambda i: (0, 0)),
           pl.BlockSpec((1,) + w_out.shape[1:], lambda i: (layer, 0, 0)),
           pl.BlockSpec((b, TM, d), lambda i: (0, i, 0)),
           pl.BlockSpec((1, 8, 3 * d), lambda i: (layer, 0, 0)),
           pl.BlockSpec((1, d), lambda i: (0, 0))],
        out_specs=pl.BlockSpec((b, TM, d), lambda i: (0, i, 0)),
        out_shape=jax.ShapeDtypeStruct((b, n, d), F32),
        compiler_params=_params(("parallel",)),
        name="outproj",
    )(pa, pd, pd, pd, sgu_w, sgu_b.T, conv_w, o_b, *ys, bonus, pc, ln_w, ln_b, w_out, x, mods,
      g_post.reshape(1, d))


def _w_in_segments():
    g = W_GRP
    o = [0]
    for s in (g, g, g, g, g, g, g, 3 * g, LORA_W + LORA_A, LORA_W + LORA_A, g, g, g, g, g):
        o.append(o[-1] + s)
    order = [0, 1, 2, 3, 4, 5, 6, 12, 13, 11, 14, 7, 10, 8, 9]
    segs, dst = [], 0
    for j in order:
        width = o[j + 1] - o[j]
        segs.append((o[j], width, dst))
        dst += -(-width // LORA_PAD) * LORA_PAD
    assert dst == N_ALL
    return segs, o[-1]


def _w_in_kernel(wt_ref, o_ref):
    segs, _ = _w_in_segments()
    cols = wt_ref.shape[2]
    for src, width, dst in segs:
        piece = wt_ref[0, src:src + width, :]
        padded = -(-width // LORA_PAD) * LORA_PAD
        if padded != width:
            piece = jnp.concatenate([piece, jnp.zeros((padded - width, cols), F32)], axis=0)
        o_ref[0, :, dst:dst + padded] = piece.T.astype(BF16)


def _regroup_w_in(w_in):
    depth, d, n_in = w_in.shape
    assert n_in == _w_in_segments()[1]
    tr = 256
    return pl.pallas_call(
        _w_in_kernel,
        grid=(depth, d // tr),
        in_specs=[pl.BlockSpec((1, n_in, tr), lambda l, i: (l, 0, i))],
        out_specs=pl.BlockSpec((1, tr, N_ALL), lambda l, i: (l, i, 0)),
        out_shape=jax.ShapeDtypeStruct((depth, d, N_ALL), BF16),
        compiler_params=_params(("parallel", "parallel")),
        name="w_in_prep",
    )(jnp.swapaxes(w_in, 1, 2))


def _cast_kernel(w_ref, o_ref):
    o_ref[...] = w_ref[...].astype(BF16)


def _cast_w_out(w_out):
    depth, k, d = w_out.shape
    tr = 512
    return pl.pallas_call(
        _cast_kernel,
        grid=(depth, k // tr),
        in_specs=[pl.BlockSpec((1, tr, d), lambda l, i: (l, i, 0))],
        out_specs=pl.BlockSpec((1, tr, d), lambda l, i: (l, i, 0)),
        out_shape=jax.ShapeDtypeStruct((depth, k, d), BF16),
        compiler_params=_params(("parallel", "parallel")),
        name="w_out_prep",
    )(w_out)


def _rope_tables(n_ctx, n_lat):
    n_rows = n_lat // GRID_W
    inv = ROPE_BASE ** (-jnp.arange(0, 32, 2, dtype=F32) / 32)
    lane = jnp.arange(HD_B)
    use_col = ((lane // 32) % 2 == 1)[None, None, :]
    trig = lambda f, n: f(jnp.arange(n, dtype=F32)[:, None] * inv[lane % 16][None, :])
    grid = lambda f: jnp.where(use_col, trig(f, GRID_W)[None, :, :], trig(f, n_rows)[:, None, :]
                               ).reshape(n_lat, HD_B)
    first = (lane % 32 < 16)[None, :]
    cos, sin = grid(jnp.cos), grid(jnp.sin)
    sa = jnp.where(first, -sin, 0.0)
    sb = jnp.where(first, 0.0, sin)
    ident = lambda v: jnp.full((n_ctx, HD_B), v, F32)
    return (jnp.concatenate([ident(1.0), cos]), jnp.concatenate([ident(0.0), sa]),
            jnp.concatenate([ident(0.0), sb]))


def _rwkv_params(l, mu, w0, w2, a0, a2, kk, ka, rk, ln_w, ln_b):
    g = W_GRP
    zpad = lambda n: jnp.zeros((2, n), F32)
    mu_l = jnp.concatenate([mu[l][:, 3 * g:], zpad(LORA_PAD - LORA_W - LORA_A)], axis=1)
    w2p = jnp.concatenate([w2[l], jnp.zeros((2, LORA_PAD - LORA_W, g), F32)], axis=1)
    a2p = jnp.concatenate([jnp.zeros((2, LORA_W, g), F32), a2[l],
                           jnp.zeros((2, LORA_PAD - LORA_W - LORA_A, g), F32)], axis=1)
    r3 = lambda v: v.reshape(2, 1, -1)
    return dict(mu_rkv=r3(mu[l][:, :3 * g]), mu_lora=r3(mu_l), w0=r3(w0[l]), w2=w2p, a0=r3(a0[l]),
                a2=a2p, kk=r3(kk[l]), ka=r3(ka[l]), rk=r3(rk[l].reshape(2, g)),
                ln_w=ln_w[l].reshape(1, g), ln_b=ln_b[l].reshape(1, g))


def kernel(x, c, ctx, c_ctx, w_mod, b_mod, g_pre, g_post, w_in, w_out, sgu_w, sgu_b, lam_q1, lam_k1, lam_q2, lam_k2, subln_g, rwkv_mu, rwkv_w0, rwkv_w2, rwkv_a0, rwkv_a2, rwkv_kk, rwkv_ka, rwkv_rk, rwkv_ln_w, rwkv_ln_b, conv_w):
    depth = w_mod.shape[0]
    b, n_lat, _ = x.shape
    n_ctx = ctx.shape[1]
    mods = _modulation(c, c_ctx, w_mod, b_mod)
    rope = _rope_tables(n_ctx, n_lat)
    w_in_b = _regroup_w_in(w_in)
    w_out_b = _cast_w_out(w_out)
    xc = ctx
    for l in range(depth):
        need_ctx = l < depth - 1
        lam_init = 0.8 - 0.6 * math.exp(-0.3 * l)
        pa, pb, pd, pc = _inproj(xc, x, mods, l, g_pre[l], w_in_b, rope)
        lam_params = jnp.stack([lam_q1[l], lam_k1[l], lam_q2[l], lam_k2[l]])
        cb = n_ctx // TM
        o_b = _attention(pb, lam_params, subln_g[l], lam_init, cb, n_lat, n_ctx + n_lat, ATT_NSUB)
        rp = _rwkv_params(l, rwkv_mu, rwkv_w0, rwkv_w2, rwkv_a0, rwkv_a2, rwkv_kk, rwkv_ka, rwkv_rk,
                          rwkv_ln_w, rwkv_ln_b)
        ys, bonus = _rwkv(pc, rp)
        local = (pa, pd, sgu_w[l], sgu_b[l], conv_w[l])
        tail = (ys, bonus, pc, rp["ln_w"], rp["ln_b"], w_out_b)
        x_new = _outproj(*local, o_b, *tail, x, mods, l, g_post[l], cb, 0, tuple(range(b)))
        if need_ctx:
            o_bc = _attention(pb, lam_params, subln_g[l], lam_init, 0, n_ctx, n_ctx, 1)
            xc = _outproj(*local, o_bc, *tail, xc, mods, l, g_post[l], 0, 0, (b,) * b)
        x = x_new
    return x
```

```python
import functools
import math

import jax
import jax.numpy as jnp
from jax import lax
from jax.experimental import pallas as pl
from jax.experimental.pallas import tpu as pltpu

F32 = jnp.float32
BF16 = jnp.bfloat16

EPS = 1e-6
GN_EPS = 64e-5
GRID_W = 64
ROPE_BASE = 10000.0
LOG2E = 1.4426950408889634

W_GRP = 512
CHUNK_A = 128
HD_B = 128
HD_QK = 64
HD_C = 64
LORA_W = 64
LORA_A = 32
LORA_PAD = 128
RW_L = 64
RW_G = 256
TM = 256
ATT_KB = 256
ATT_NSUB = 4
ATT_CHAIN = 2
VMEM_LIMIT = 56 * 1024 * 1024

N_A = 3 * W_GRP
N_B = 4 * W_GRP
N_D = 4 * W_GRP
N_C = 4 * W_GRP + 2 * LORA_PAD
OFF_A, OFF_B, OFF_D, OFF_C = 0, N_A, N_A + N_B, N_A + N_B + N_D
N_ALL = OFF_C + N_C


def _silu(g):
    return g * (1.0 / (1.0 + jnp.exp(-g)))


def _sigmoid(x):
    return 1.0 / (1.0 + jnp.exp(-x))


def _split_bf16(x):
    hi = x.astype(BF16)
    lo = (x - hi.astype(F32)).astype(BF16)
    return hi, lo


def _mm(a, b):
    return jnp.dot(a, b, preferred_element_type=F32)


def _mm_nt(a, b):
    return lax.dot_general(a, b, (((1,), (1,)), ((), ())), preferred_element_type=F32)


def _params(sem):
    return pltpu.CompilerParams(dimension_semantics=sem, vmem_limit_bytes=VMEM_LIMIT)


def _mod_kernel(s_ref, w_ref, b_ref, o_ref):
    s = _silu(s_ref[...])
    o_ref[0] = _mm(s.astype(BF16), w_ref[0].astype(BF16)) + b_ref[0]


def _modulation(c, c_ctx, w_mod, b_mod):
    depth, d, n3 = w_mod.shape
    b = c.shape[0]
    rows = jnp.zeros((8, d), F32).at[:b].set(c).at[b].set(c_ctx)
    tn = 768
    return pl.pallas_call(
        _mod_kernel,
        grid=(depth, n3 // tn),
        in_specs=[pl.BlockSpec((8, d), lambda l, j: (0, 0)),
                  pl.BlockSpec((1, d, tn), lambda l, j: (l, 0, j)),
                  pl.BlockSpec((1, 1, tn), lambda l, j: (l, 0, j))],
        out_specs=pl.BlockSpec((1, 8, tn), lambda l, j: (l, 0, j)),
        out_shape=jax.ShapeDtypeStruct((depth, 8, n3), F32),
        compiler_params=_params(("parallel", "parallel")),
        name="modulation",
    )(rows, w_mod, b_mod.reshape(depth, 1, n3))


def _inproj_kernel(n_batch, xc_ref, xl_ref, mod_ref, g_ref, w_ref, cos_ref, sa_ref, sb_ref,
                   pa_ref, pb_ref, pd_ref, pc_ref):
    b = pl.program_id(0)
    i = pl.program_id(1)
    d = xc_ref.shape[-1]
    is_ctx = i == 0
    x = jnp.where(is_ctx, xc_ref[0], xl_ref[0])
    y = x * lax.rsqrt(jnp.mean(x * x, axis=-1, keepdims=True) + EPS) * g_ref[...]
    row = jnp.where(is_ctx, mod_ref[0, pl.ds(n_batch, 1), :], mod_ref[0, pl.ds(b, 1), :])
    h = (y * (1.0 + row[:, d:2 * d]) + row[:, :d]).astype(BF16)

    def proj(c0, c1):
        return _mm(h, w_ref[0, :, c0:c1])

    pa_ref[0] = proj(OFF_A, OFF_A + N_A).astype(BF16)
    cos, sa, sb = cos_ref[...], sa_ref[...], sb_ref[...]
    qk = proj(OFF_B, OFF_B + 2 * W_GRP)
    for j in range(2 * W_GRP // HD_B):
        t = qk[:, j * HD_B:(j + 1) * HD_B]
        t = t * cos + pltpu.roll(t, HD_B - 16, axis=1) * sa + pltpu.roll(t, 16, axis=1) * sb
        if j < W_GRP // HD_B:
            t = t * (HD_QK ** -0.5 * LOG2E)
        pb_ref[0, :, j * HD_B:(j + 1) * HD_B] = t.astype(BF16)
    pb_ref[0, :, 2 * W_GRP:] = proj(OFF_B + 2 * W_GRP, OFF_B + N_B).astype(BF16)
    pd_ref[0] = proj(OFF_D, OFF_D + N_D).astype(BF16)
    pc_ref[0] = proj(OFF_C, OFF_C + N_C).astype(BF16)


def _inproj(xc, xl, mods, layer, g_pre, w_in, rope):
    b, n_ctx, d = xc.shape
    n_lat = xl.shape[1]
    t = n_ctx + n_lat
    assert n_ctx == TM and n_lat % TM == 0
    nblk = t // TM
    cos, sa, sb = rope
    tab = pl.BlockSpec((TM, HD_B), lambda bb, i: (i, 0))
    out = lambda n: pl.BlockSpec((1, TM, n), lambda bb, i: (bb, i, 0))
    return pl.pallas_call(
        functools.partial(_inproj_kernel, b),
        grid=(b, nblk),
        in_specs=[pl.BlockSpec((1, TM, d), lambda bb, i: (bb, 0, 0)),
                  pl.BlockSpec((1, TM, d), lambda bb, i: (bb, jnp.maximum(i - 1, 0), 0)),
                  pl.BlockSpec((1, 8, 3 * d), lambda bb, i: (layer, 0, 0)),
                  pl.BlockSpec((1, d), lambda bb, i: (0, 0)),
                  pl.BlockSpec((1, d, N_ALL), lambda bb, i: (layer, 0, 0), pipeline_mode=pl.Buffered(1)),
                  tab, tab, tab],
        out_specs=[out(N_A), out(N_B), out(N_D), out(N_C)],
        out_shape=[jax.ShapeDtypeStruct((b, t, n), BF16) for n in (N_A, N_B, N_D, N_C)],
        compiler_params=_params(("parallel", "arbitrary")),
        name="inproj",
    )(xc, xl, mods, g_pre.reshape(1, d), w_in, cos, sa, sb)


def _attn_kernel(lam_init, nsub, *refs):
    q_refs, (k_ref, v_ref), g_refs = refs[:nsub], refs[nsub:nsub + 2], refs[nsub + 2:2 * nsub + 2]
    lam_ref, sg_ref, o_ref, vext_ref = refs[2 * nsub + 2:]
    tq = nsub * TM
    n_keys = k_ref.shape[1]

    @pl.when(pl.program_id(2) == 0)
    def _():
        vext_ref[:, :HD_B] = v_ref[0]
        vext_ref[:, HD_B:] = jnp.ones((n_keys, HD_B), BF16)

    lp = lam_ref[...]
    lam = (jnp.exp(jnp.sum(lp[0:1] * lp[1:2], axis=-1, keepdims=True))
           - jnp.exp(jnp.sum(lp[2:3] * lp[3:4], axis=-1, keepdims=True)) + lam_init)
    lane = lax.broadcasted_iota(jnp.int32, (1, HD_B), 1)
    n_chain = -(-nsub // ATT_CHAIN)
    tc = tq // n_chain
    q2s = []
    for ci in range(n_chain):
        q = jnp.concatenate([r[0] for r in q_refs[ci * ATT_CHAIN:(ci + 1) * ATT_CHAIN]], axis=0)
        zero = jnp.zeros_like(q)
        q2s.append(jnp.concatenate([jnp.where(lane < HD_QK, q, zero),
                                    jnp.where(lane >= HD_QK, q, zero)], axis=0))
    ms = [None] * n_chain
    os_ = [None] * n_chain
    for k0 in range(0, n_keys, ATT_KB):
        k1 = min(k0 + ATT_KB, n_keys)
        for ci in range(n_chain):
            s = _mm_nt(q2s[ci], k_ref[0, k0:k1])
            bm = jnp.max(s, axis=-1, keepdims=True)
            if ms[ci] is None:
                ms[ci] = bm
                os_[ci] = _mm(jnp.exp2(s - bm).astype(BF16), vext_ref[k0:k1])
            else:
                m_new = jnp.maximum(ms[ci], bm)
                os_[ci] = (os_[ci] * jnp.exp2(ms[ci] - m_new)
                           + _mm(jnp.exp2(s - m_new).astype(BF16), vext_ref[k0:k1]))
                ms[ci] = m_new
    for ci in range(n_chain):
        o = os_[ci]
        a = o[:tc, :HD_B] / o[:tc, HD_B:] - lam * (o[tc:, :HD_B] / o[tc:, HD_B:])
        y = a * lax.rsqrt(jnp.mean(a * a, axis=-1, keepdims=True) + EPS) * sg_ref[...] * (1.0 - lam_init)
        g = jnp.concatenate([r[0] for r in g_refs[ci * ATT_CHAIN:(ci + 1) * ATT_CHAIN]], axis=0)
        o_ref[0, ci * tc:(ci + 1) * tc, :] = (y * _silu(g.astype(F32))).astype(BF16)


def _attention(pb, lam_params, subln_g, lam_init, q_blk0, n_q, n_keys, nsub):
    b = pb.shape[0]
    nh = W_GRP // HD_B
    assert n_q % (nsub * TM) == 0
    q_spec = lambda col0, j: pl.BlockSpec(
        (1, TM, HD_B), lambda bb, h, i: (bb, nsub * i + q_blk0 + j, col0 + h))
    kv_spec = lambda col0: pl.BlockSpec((1, n_keys, HD_B), lambda bb, h, i: (bb, 0, col0 + h))
    return pl.pallas_call(
        functools.partial(_attn_kernel, lam_init, nsub),
        grid=(b, nh, n_q // (nsub * TM)),
        in_specs=([q_spec(0, j) for j in range(nsub)] + [kv_spec(nh), kv_spec(2 * nh)]
                  + [q_spec(3 * nh, j) for j in range(nsub)]
                  + [pl.BlockSpec((4, HD_QK), lambda bb, h, i: (0, 0)),
                     pl.BlockSpec((1, HD_B), lambda bb, h, i: (0, 0))]),
        out_specs=pl.BlockSpec((1, nsub * TM, HD_B), lambda bb, h, i: (bb, i, h)),
        out_shape=jax.ShapeDtypeStruct((b, n_q, W_GRP), BF16),
        scratch_shapes=[pltpu.VMEM((n_keys, 2 * HD_B), BF16)],
        compiler_params=_params(("parallel", "parallel", "arbitrary")),
        name="diff_attention",
    )(*([pb] * (2 * nsub + 2)), lam_params, subln_g.reshape(1, HD_B))


def _seg_sum(x, ones_bd):
    hi, lo = _split_bf16(x)
    return _mm(hi, ones_bd) + _mm(lo, ones_bd)


def _head_diag(x4, mask):
    return jnp.where(mask, x4, 0.0).astype(BF16)


def _rwkv_prep_kernel(n_blk, rkv_ref, lo_ref, hrkv_ref, hlo_ref, mu_ref, mul_ref, w0_ref, w2_ref,
                      a0_ref, a2_ref, kk_ref, ka_ref, rk_ref,
                      g_ref, q_ref, rp_ref, yp_ref, bonus_ref):
    dr = pl.program_id(1)
    i = pl.program_id(2)
    fwd = dr == 0
    nl = RW_G // HD_C
    sgn = 1 - 2 * dr
    hrow = jnp.where(fwd, 15, 0)
    edge = jnp.logical_or(i == 0, i == jnp.where(fwd, 1, n_blk - 1))
    rsel = lax.broadcasted_iota(jnp.int32, (TM, 1), 0) == jnp.where(fwd, 0, TM - 1)
    hsel = lax.broadcasted_iota(jnp.int32, (16, 1), 0) == hrow

    def shifted(cur_ref, halo_ref, mu):
        f = cur_ref[0].astype(F32)
        halo = jnp.sum(jnp.where(hsel, halo_ref[0].astype(F32), 0.0), axis=0, keepdims=True)
        halo = jnp.where(edge, 0.0, halo)
        rolled = jnp.where(fwd, pltpu.roll(f, 1, axis=0), pltpu.roll(f, TM - 1, axis=0))
        prev = jnp.where(rsel, halo, rolled)
        return f + (prev - f) * mu

    z = shifted(rkv_ref, hrkv_ref, mu_ref[0])
    zl = shifted(lo_ref, hlo_ref, mul_ref[0])
    r, k, v = z[:, :W_GRP], z[:, W_GRP:2 * W_GRP], z[:, 2 * W_GRP:]
    wl = w0_ref[0] + _mm(jnp.tanh(zl).astype(BF16), w2_ref[0].astype(BF16))
    logd = -math.exp(-0.5) * _sigmoid(wl)
    a = _sigmoid(a0_ref[0] + _mm(zl.astype(BF16), a2_ref[0].astype(BF16)))
    ri = lax.broadcasted_iota(jnp.int32, (W_GRP, W_GRP), 0) // HD_C
    ci = lax.broadcasted_iota(jnp.int32, (W_GRP, W_GRP), 1) // HD_C
    ones_bd = jnp.where(ri == ci, 1.0, 0.0).astype(BF16)
    kk = k * kk_ref[0]
    kk = kk / jnp.maximum(jnp.sqrt(_seg_sum(kk * kk, ones_bd)), 1e-12)
    k2 = k * (1.0 + (a - 1.0) * ka_ref[0])
    bv = kk * a
    bonus_ref[0, 0] = (_seg_sum(r * k2 * rk_ref[0], ones_bd) * v).astype(BF16)
    rt = lax.broadcasted_iota(jnp.int32, (TM, TM), 0)
    ct = lax.broadcasted_iota(jnp.int32, (TM, TM), 1)
    same = rt // RW_L == ct // RW_L
    tri = jnp.where(jnp.logical_and(same, (rt - ct) * sgn >= 0), 1.0, 0.0).astype(BF16)
    blk = jnp.where(same, 1.0, 0.0).astype(BF16)
    lh, ll = _split_bf16(logd)
    cum = _mm(tri, lh) + _mm(tri, ll)
    tot = _mm(blk, lh) + _mm(blk, ll)
    half = 0.5 * tot
    e_pos = jnp.exp(cum - half)
    e_neg = jnp.exp(half - cum)
    e_prev = jnp.exp(cum - logd - half)
    e_half = jnp.exp(half)
    e_end = e_half * e_neg
    g_end = e_half * e_half
    kt_all = kk * e_prev
    rt_all = r * e_pos
    kh_all = k2 * e_neg
    bh_all = bv * e_neg
    kb_all = k2 * e_end
    bb_all = bv * e_end
    row = lax.broadcasted_iota(jnp.int32, (RW_L, RW_G), 0)
    col = lax.broadcasted_iota(jnp.int32, (RW_L, RW_G), 1) % HD_C
    eye = jnp.where(col == row, 1.0, 0.0)
    r4 = lax.broadcasted_iota(jnp.int32, (RW_G, RW_G), 0) // HD_C
    c4 = lax.broadcasted_iota(jnp.int32, (RW_G, RW_G), 1) // HD_C
    dmask = r4 == c4
    r2 = lax.broadcasted_iota(jnp.int32, (2 * RW_L, RW_G), 0) % RW_L
    c2 = lax.broadcasted_iota(jnp.int32, (2 * RW_L, RW_G), 1) % HD_C
    first = jnp.where(lax.broadcasted_iota(jnp.int32, (2 * RW_L, RW_G), 0) < RW_L, 1, 0)
    mask2 = (r2 - c2) * sgn >= first

    def bd(x):
        return _head_diag(jnp.concatenate([x] * nl, axis=0), dmask)

    def hprod(x, y):
        return _mm(x.astype(BF16), bd(y))

    def diag_blocks(full):
        out = jnp.zeros((RW_L, RW_G), F32)
        lane_h = lax.broadcasted_iota(jnp.int32, (RW_L, RW_G), 1) // HD_C
        for hh in range(nl):
            out = out + jnp.where(lane_h == hh, full[hh * HD_C:(hh + 1) * HD_C, :], 0.0)
        return out

    pairs = [(slice(c * RW_L, (c + 1) * RW_L), slice(gi * RW_G, (gi + 1) * RW_G))
             for c in range(TM // RW_L) for gi in range(W_GRP // RW_G)]
    cat = lambda *xs: jnp.concatenate(xs, axis=0)
    L = RW_L
    kt = [kt_all[p] for p in pairs]
    rtl = [rt_all[p] for p in pairs]
    vv = [v[p] for p in pairs]
    lhs = [cat(a_, b_).astype(BF16) for a_, b_ in zip(kt, rtl)]
    pb_ = [jnp.where(mask2, _mm_nt(x, bd(bh_all[p])), 0.0) for x, p in zip(lhs, pairs)]
    pk_ = [jnp.where(mask2, _mm_nt(x, bd(kh_all[p])), 0.0) for x, p in zip(lhs, pairs)]
    res = [hprod(x, x[:L]) for x in pb_]
    tinv = [eye - x[:L] for x in pb_]
    mt = [x[L:] - y[L:] for x, y in zip(pb_, res)]
    pw = [y[:L] for y in res]
    for _ in range(4):
        res = [hprod(cat(t_, m_, w_), w_) for t_, m_, w_ in zip(tinv, mt, pw)]
        tinv = [t_ + y[:L] for t_, y in zip(tinv, res)]
        mt = [m_ + y[L:2 * L] for m_, y in zip(mt, res)]
        pw = [y[2 * L:] for y in res]
    res = [hprod(cat(t_, m_), w_) for t_, m_, w_ in zip(tinv, mt, pw)]
    tm = [cat(t_, m_) + y for t_, m_, y in zip(tinv, mt, res)]
    wv = [hprod(x, y) for x, y in zip(pk_, vv)]
    tmb = [x.astype(BF16) for x in tm]
    ww = [_mm(x, bd(y)) for x, y in zip(tmb, kt)]
    uu = [_mm(x, bd(y[:L])) for x, y in zip(tmb, wv)]
    for j, p in enumerate(pairs):
        rows, cols = p
        eh = e_half[rows.start:rows.start + 1, cols]
        rp_ref[0, 0, rows, cols] = ((rtl[j] - ww[j][L:]) * eh).astype(BF16)
        yp_ref[0, 0, rows, cols] = (wv[j][L:] - uu[j][L:]).astype(BF16)
        xt = cat(kb_all[p], -bb_all[p]).T.astype(BF16)
        zeros = jnp.zeros((L, RW_G), F32)
        yq = cat(vv[j], uu[j][:L]).astype(BF16)
        yg = cat(zeros, ww[j][:L]).astype(BF16)
        q_ref[0, 0, rows, cols] = diag_blocks(_mm(xt, yq))
        g_ref[0, 0, rows, cols] = (diag_blocks(_mm(xt, yg)) * eh
                                   + eye * g_end[rows.start:rows.start + 1, cols])


def _rwkv_scan_kernel(n_chain, *refs):
    ins, y_refs, h_ref = refs[:4 * n_chain], refs[4 * n_chain:5 * n_chain], refs[-1]
    nl = RW_G // HD_C

    @pl.when(pl.program_id(0) == 0)
    def _():
        h_ref[...] = jnp.zeros_like(h_ref)

    r4 = lax.broadcasted_iota(jnp.int32, (RW_G, RW_G), 0) // HD_C
    c4 = lax.broadcasted_iota(jnp.int32, (RW_G, RW_G), 1) // HD_C
    dmask = r4 == c4
    n_chunk = TM // RW_L
    links = [(j, slice(gi * RW_G, (gi + 1) * RW_G)) for j in range(n_chain)
             for gi in range(W_GRP // RW_G)]
    h = [h_ref[j, :, cols] for j, cols in links]
    for s in range(n_chunk):
        prods = []
        for (j, cols), hj in zip(links, h):
            g_ref, _, rp_ref, _ = ins[4 * j:4 * j + 4]
            c = n_chunk - 1 - s if j % 2 else s
            rows = slice(c * RW_L, (c + 1) * RW_L)
            h4 = jnp.where(dmask, jnp.concatenate([hj] * nl, axis=0), 0.0)
            hh, hl = _split_bf16(h4)
            gh, gl = _split_bf16(g_ref[0, 0, rows, cols])
            lh = jnp.concatenate([gh, rp_ref[0, 0, rows, cols]], axis=0)
            prods.append((_mm(lh, hh) + _mm(lh, hl), _mm(gl, hh)))
        for k, ((j, cols), (prod, corr)) in enumerate(zip(links, prods)):
            _, q_ref, _, yp_ref = ins[4 * j:4 * j + 4]
            c = n_chunk - 1 - s if j % 2 else s
            rows = slice(c * RW_L, (c + 1) * RW_L)
            y_refs[j][0, 0, rows, cols] = (prod[RW_L:] + yp_ref[0, 0, rows, cols].astype(F32)).astype(BF16)
            h[k] = prod[:RW_L] + corr + q_ref[0, 0, rows, cols]
    for (j, cols), hj in zip(links, h):
        h_ref[j, :, cols] = hj


def _scan_block(dr, i, n_blk):
    return jnp.where(dr == 0, i, jnp.where(i == 0, 0, n_blk - i))


def _rwkv(pc, p):
    b, t, _ = pc.shape
    nblk = t // TM
    hb = TM // 16
    lblk = 4 * W_GRP // LORA_PAD

    def halo_idx(dr, i):
        return jnp.where(dr == 0, jnp.maximum(i * hb - 1, 0), jnp.minimum((i + 1) * hb, nblk * hb - 1))

    per_dir = lambda n: pl.BlockSpec((1, 1, n), lambda bb, dr, i: (dr, 0, 0))
    per_dir_m = lambda m, n: pl.BlockSpec((1, m, n), lambda bb, dr, i: (dr, 0, 0))
    state = lambda: pl.BlockSpec((1, 1, TM, W_GRP), lambda bb, dr, i: (bb, dr, i, 0))
    sds = lambda dt: jax.ShapeDtypeStruct((b, 2, t, W_GRP), dt)
    gm, qm, rp, yp, bonus = pl.pallas_call(
        functools.partial(_rwkv_prep_kernel, nblk),
        grid=(b, 2, nblk),
        in_specs=[pl.BlockSpec((1, TM, 3 * W_GRP), lambda bb, dr, i: (bb, i, 0)),
                  pl.BlockSpec((1, TM, LORA_PAD), lambda bb, dr, i: (bb, i, lblk + dr)),
                  pl.BlockSpec((1, 16, 3 * W_GRP), lambda bb, dr, i: (bb, halo_idx(dr, i), 0)),
                  pl.BlockSpec((1, 16, LORA_PAD), lambda bb, dr, i: (bb, halo_idx(dr, i), lblk + dr)),
                  per_dir(3 * W_GRP), per_dir(LORA_PAD), per_dir(W_GRP), per_dir_m(LORA_PAD, W_GRP),
                  per_dir(W_GRP), per_dir_m(LORA_PAD, W_GRP), per_dir(W_GRP), per_dir(W_GRP),
                  per_dir(W_GRP)],
        out_specs=[state()] * 5,
        out_shape=[sds(F32), sds(F32), sds(BF16), sds(BF16), sds(BF16)],
        compiler_params=_params(("parallel", "parallel", "parallel")),
        name="rwkv_prep",
    )(pc, pc, pc, pc, p["mu_rkv"], p["mu_lora"], p["w0"], p["w2"], p["a0"], p["a2"],
      p["kk"], p["ka"], p["rk"])

    chains = [(bb, dr) for bb in range(b) for dr in range(2)]
    scan_spec = lambda bb, dr: pl.BlockSpec(
        (1, 1, TM, W_GRP), lambda i: (bb, dr, _scan_block(dr, i, nblk), 0))
    sds1 = jax.ShapeDtypeStruct((1, 1, t, W_GRP), BF16)
    ys = pl.pallas_call(
        functools.partial(_rwkv_scan_kernel, len(chains)),
        grid=(nblk,),
        in_specs=[scan_spec(bb, dr) for bb, dr in chains for _ in range(4)],
        out_specs=[pl.BlockSpec((1, 1, TM, W_GRP), lambda i, dr=dr: (0, 0, _scan_block(dr, i, nblk), 0))
                   for _, dr in chains],
        out_shape=[sds1] * len(chains),
        scratch_shapes=[pltpu.VMEM((len(chains), RW_L, W_GRP), F32)],
        compiler_params=_params(("arbitrary",)),
        name="rwkv_scan",
    )(*[a for _ in chains for a in (gm, qm, rp, yp)])
    return ys, bonus


def _sgu(pa_ref, bb, ws_ref, bs_ref):
    rows_out = []
    for c in range(TM // CHUNK_A):
        rows = slice(c * CHUNK_A, (c + 1) * CHUNK_A)
        heads = []
        for h in range(W_GRP // CHUNK_A):
            cols = lambda k: slice(k * W_GRP + h * CHUNK_A, k * W_GRP + (h + 1) * CHUNK_A)
            u = pa_ref[bb, rows, cols(0)].astype(F32)
            v = pa_ref[bb, rows, cols(1)].astype(F32)
            g = pa_ref[bb, rows, cols(2)].astype(F32)
            vc = v - jnp.mean(v, axis=-1, keepdims=True)
            vn = vc * lax.rsqrt(jnp.mean(vc * vc, axis=-1, keepdims=True) + EPS)
            mixed = _mm(ws_ref[h].astype(BF16), vn.astype(BF16)) + bs_ref[:, h:h + 1]
            heads.append((u * mixed * _silu(g)).astype(BF16))
        rows_out.append(jnp.concatenate(heads, axis=1))
    return jnp.concatenate(rows_out, axis=0)


def _short_conv(pd_ref, hp_ref, hn_ref, bb, cw_ref, first, last):
    cx = pd_ref[bb, :, 0:2 * W_GRP].astype(F32)
    z = cx[:, :W_GRP] * cx[:, W_GRP:]
    hp = hp_ref[bb].astype(F32)
    hn = hn_ref[bb].astype(F32)
    zp = jnp.where(first, 0.0, hp[15:16, :W_GRP] * hp[15:16, W_GRP:])
    zn = jnp.where(last, 0.0, hn[0:1, :W_GRP] * hn[0:1, W_GRP:])
    r = lax.broadcasted_iota(jnp.int32, (TM, 1), 0)
    z_prev = jnp.where(r == 0, zp, pltpu.roll(z, 1, axis=0))
    z_next = jnp.where(r == TM - 1, zn, pltpu.roll(z, TM - 1, axis=0))
    w = cw_ref[...]
    y = z_prev * w[0:1] + z * w[1:2] + z_next * w[2:3]
    bg = pd_ref[bb, :, 2 * W_GRP:3 * W_GRP].astype(F32)
    g = pd_ref[bb, :, 3 * W_GRP:].astype(F32)
    return (bg * y * _silu(g)).astype(BF16)


def _outproj_kernel(mod_rows, blk0, n_blk, *refs):
    n_batch = len(mod_rows)
    pa_ref, pd_ref, hp_ref, hn_ref, ws_ref, bs_ref, cw_ref, ob_ref = refs[:8]
    y_refs = refs[8:8 + 2 * n_batch]
    bonus_ref, gc_ref, lw_ref, lb_ref, w_ref, x_ref, mod_ref, g_ref, o_ref = refs[8 + 2 * n_batch:]
    d = x_ref.shape[-1]
    ib = pl.program_id(0) + blk0
    first = ib <= 1
    last = jnp.logical_or(ib == 0, ib == n_blk - 1)
    ri = lax.broadcasted_iota(jnp.int32, (W_GRP, W_GRP), 0) // HD_C
    ci = lax.broadcasted_iota(jnp.int32, (W_GRP, W_GRP), 1) // HD_C
    ones_bd = jnp.where(ri == ci, 1.0, 0.0).astype(BF16)
    for bb in range(n_batch):
        oa = _sgu(pa_ref, bb, ws_ref, bs_ref)
        od = _short_conv(pd_ref, hp_ref, hn_ref, bb, cw_ref, first, last)
        yc = y_refs[2 * bb][0, 0].astype(F32) + y_refs[2 * bb + 1][0, 0].astype(F32)
        yc = yc - _mm(yc.astype(BF16), ones_bd) * (1.0 / HD_C)
        var = _mm((yc * yc).astype(BF16), ones_bd) * (1.0 / HD_C)
        yn = yc * lax.rsqrt(var + GN_EPS) * lw_ref[...] + lb_ref[...]
        oc = yn + bonus_ref[bb, 0].astype(F32) + bonus_ref[bb, 1].astype(F32)
        oc = (oc * _silu(gc_ref[bb].astype(F32))).astype(BF16)
        mix = jnp.concatenate([oa, ob_ref[bb], oc, od], axis=1)
        acc = _mm(mix, w_ref[0])
        y = acc * lax.rsqrt(jnp.mean(acc * acc, axis=-1, keepdims=True) + EPS) * g_ref[...]
        o_ref[bb] = x_ref[bb] + mod_ref[0, mod_rows[bb]:mod_rows[bb] + 1, 2 * d:] * y


def _outproj(pa, pd, sgu_w, sgu_b, conv_w, o_b, ys, bonus, pc, ln_w, ln_b, w_out, x, mods, layer,
             g_post, blk0, ob_blk0, mod_rows):
    b, n, d = x.shape
    t = pa.shape[1]
    nblk = t // TM
    hb = TM // 16
    tok = lambda off, col, width: pl.BlockSpec((b, TM, width), lambda i: (0, i + off, col))
    return pl.pallas_call(
        functools.partial(_outproj_kernel, mod_rows, blk0, nblk),
        grid=(n // TM,),
        in_specs=[tok(blk0, 0, N_A), tok(blk0, 0, N_D),
                  pl.BlockSpec((b, 16, 2 * W_GRP),
                               lambda i: (0, jnp.maximum((i + blk0) * hb - 1, 0), 0)),
                  pl.BlockSpec((b, 16, 2 * W_GRP),
                               lambda i: (0, jnp.minimum((i + blk0 + 1) * hb, nblk * hb - 1), 0)),
                  pl.BlockSpec(sgu_w.shape, lambda i: (0, 0, 0)),
                  pl.BlockSpec((CHUNK_A, sgu_w.shape[0]), lambda i: (0, 0)),
                  pl.BlockSpec(conv_w.shape, lambda i: (0, 0)),
                  tok(ob_blk0, 0, W_GRP)]
        + [pl.BlockSpec((1, 1, TM, W_GRP), lambda i: (0, 0, i + blk0, 0))] * len(ys)
        + [pl.BlockSpec((b, 2, TM, W_GRP), lambda i: (0, 0, i + blk0, 0)),
           tok(blk0, 3, W_GRP),
           pl.BlockSpec((1, W_GRP), lambda i: (0, 0)),
           pl.BlockSpec((1, W_GRP), lambda i: (0, 0)),
           pl.BlockSpec((1,) + w_out.shape[1:], lambda i: (layer, 0, 0)),
           pl.BlockSpec((b, TM, d), lambda i: (0, i, 0)),
           pl.BlockSpec((1, 8, 3 * d), lambda i: (layer, 0, 0)),
           pl.BlockSpec((1, d), lambda i: (0, 0))],
        out_specs=pl.BlockSpec((b, TM, d), lambda i: (0, i, 0)),
        out_shape=jax.ShapeDtypeStruct((b, n, d), F32),
        compiler_params=_params(("parallel",)),
        name="outproj",
    )(pa, pd, pd, pd, sgu_w, sgu_b.T, conv_w, o_b, *ys, bonus, pc, ln_w, ln_b, w_out, x, mods,
      g_post.reshape(1, d))


def _w_in_segments():
    g = W_GRP
    o = [0]
    for s in (g, g, g, g, g, g, g, 3 * g, LORA_W + LORA_A, LORA_W + LORA_A, g, g, g, g, g):
        o.append(o[-1] + s)
    order = [0, 1, 2, 3, 4, 5, 6, 12, 13, 11, 14, 7, 10, 8, 9]
    segs, dst = [], 0
    for j in order:
        width = o[j + 1] - o[j]
        segs.append((o[j], width, dst))
        dst += -(-width // LORA_PAD) * LORA_PAD
    assert dst == N_ALL
    return segs, o[-1]


def _w_in_kernel(wt_ref, o_ref):
    segs, _ = _w_in_segments()
    cols = wt_ref.shape[2]
    for src, width, dst in segs:
        piece = wt_ref[0, src:src + width, :]
        padded = -(-width // LORA_PAD) * LORA_PAD
        if padded != width:
            piece = jnp.concatenate([piece, jnp.zeros((padded - width, cols), F32)], axis=0)
        o_ref[0, :, dst:dst + padded] = piece.T.astype(BF16)


def _regroup_w_in(w_in):
    depth, d, n_in = w_in.shape
    assert n_in == _w_in_segments()[1]
    tr = 256
    return pl.pallas_call(
        _w_in_kernel,
        grid=(depth, d // tr),
        in_specs=[pl.BlockSpec((1, n_in, tr), lambda l, i: (l, 0, i))],
        out_specs=pl.BlockSpec((1, tr, N_ALL), lambda l, i: (l, i, 0)),
        out_shape=jax.ShapeDtypeStruct((depth, d, N_ALL), BF16),
        compiler_params=_params(("parallel", "parallel")),
        name="w_in_prep",
    )(jnp.swapaxes(w_in, 1, 2))


def _cast_kernel(w_ref, o_ref):
    o_ref[...] = w_ref[...].astype(BF16)


def _cast_w_out(w_out):
    depth, k, d = w_out.shape
    tr = 512
    return pl.pallas_call(
        _cast_kernel,
        grid=(depth, k // tr),
        in_specs=[pl.BlockSpec((1, tr, d), lambda l, i: (l, i, 0))],
        out_specs=pl.BlockSpec((1, tr, d), lambda l, i: (l, i, 0)),
        out_shape=jax.ShapeDtypeStruct((depth, k, d), BF16),
        compiler_params=_params(("parallel", "parallel")),
        name="w_out_prep",
    )(w_out)


def _rope_tables(n_ctx, n_lat):
    n_rows = n_lat // GRID_W
    inv = ROPE_BASE ** (-jnp.arange(0, 32, 2, dtype=F32) / 32)
    lane = jnp.arange(HD_B)
    use_col = ((lane // 32) % 2 == 1)[None, None, :]
    trig = lambda f, n: f(jnp.arange(n, dtype=F32)[:, None] * inv[lane % 16][None, :])
    grid = lambda f: jnp.where(use_col, trig(f, GRID_W)[None, :, :], trig(f, n_rows)[:, None, :]
                               ).reshape(n_lat, HD_B)
    first = (lane % 32 < 16)[None, :]
    cos, sin = grid(jnp.cos), grid(jnp.sin)
    sa = jnp.where(first, -sin, 0.0)
    sb = jnp.where(first, 0.0, sin)
    ident = lambda v: jnp.full((n_ctx, HD_B), v, F32)
    return (jnp.concatenate([ident(1.0), cos]), jnp.concatenate([ident(0.0), sa]),
            jnp.concatenate([ident(0.0), sb]))


def _rwkv_params(l, mu, w0, w2, a0, a2, kk, ka, rk, ln_w, ln_b):
    g = W_GRP
    zpad = lambda n: jnp.zeros((2, n), F32)
    mu_l = jnp.concatenate([mu[l][:, 3 * g:], zpad(LORA_PAD - LORA_W - LORA_A)], axis=1)
    w2p = jnp.concatenate([w2[l], jnp.zeros((2, LORA_PAD - LORA_W, g), F32)], axis=1)
    a2p = jnp.concatenate([jnp.zeros((2, LORA_W, g), F32), a2[l],
                           jnp.zeros((2, LORA_PAD - LORA_W - LORA_A, g), F32)], axis=1)
    r3 = lambda v: v.reshape(2, 1, -1)
    return dict(mu_rkv=r3(mu[l][:, :3 * g]), mu_lora=r3(mu_l), w0=r3(w0[l]), w2=w2p, a0=r3(a0[l]),
                a2=a2p, kk=r3(kk[l]), ka=r3(ka[l]), rk=r3(rk[l].reshape(2, g)),
                ln_w=ln_w[l].reshape(1, g), ln_b=ln_b[l].reshape(1, g))


def kernel(x, c, ctx, c_ctx, w_mod, b_mod, g_pre, g_post, w_in, w_out, sgu_w, sgu_b, lam_q1, lam_k1, lam_q2, lam_k2, subln_g, rwkv_mu, rwkv_w0, rwkv_w2, rwkv_a0, rwkv_a2, rwkv_kk, rwkv_ka, rwkv_rk, rwkv_ln_w, rwkv_ln_b, conv_w):
    depth = w_mod.shape[0]
    b, n_lat, _ = x.shape
    n_ctx = ctx.shape[1]
    mods = _modulation(c, c_ctx, w_mod, b_mod)
    rope = _rope_tables(n_ctx, n_lat)
    w_in_b = _regroup_w_in(w_in)
    w_out_b = _cast_w_out(w_out)
    xc = ctx
    for l in range(depth):
        need_ctx = l < depth - 1
        lam_init = 0.8 - 0.6 * math.exp(-0.3 * l)
        pa, pb, pd, pc = _inproj(xc, x, mods, l, g_pre[l], w_in_b, rope)
        lam_params = jnp.stack([lam_q1[l], lam_k1[l], lam_q2[l], lam_k2[l]])
        cb = n_ctx // TM
        o_b = _attention(pb, lam_params, subln_g[l], lam_init, cb, n_lat, n_ctx + n_lat, ATT_NSUB)
        rp = _rwkv_params(l, rwkv_mu, rwkv_w0, rwkv_w2, rwkv_a0, rwkv_a2, rwkv_kk, rwkv_ka, rwkv_rk,
                          rwkv_ln_w, rwkv_ln_b)
        ys, bonus = _rwkv(pc, rp)
        local = (pa, pd, sgu_w[l], sgu_b[l], conv_w[l])
        tail = (ys, bonus, pc, rp["ln_w"], rp["ln_b"], w_out_b)
        x_new = _outproj(*local, o_b, *tail, x, mods, l, g_post[l], cb, 0, tuple(range(b)))
        if need_ctx:
            o_bc = _attention(pb, lam_params, subln_g[l], lam_init, 0, n_ctx, n_ctx, 1)
            xc = _outproj(*local, o_bc, *tail, xc, mods, l, g_post[l], 0, 0, (b,) * b)
        x = x_new
    return x
```

```python
import functools
import math

import jax
import jax.numpy as jnp
from jax import lax
from jax.experimental import pallas as pl
from jax.experimental.pallas import tpu as pltpu

F32 = jnp.float32
BF16 = jnp.bfloat16

EPS = 1e-6
GN_EPS = 64e-5
GRID_W = 64
ROPE_BASE = 10000.0
LOG2E = 1.4426950408889634

W_GRP = 512
CHUNK_A = 128
HD_B = 128
HD_QK = 64
HD_C = 64
LORA_W = 64
LORA_A = 32
LORA_PAD = 128
RW_L = 64
RW_G = 256
TM = 256
ATT_KB = 256
ATT_NSUB = 4
ATT_CHAIN = 2
VMEM_LIMIT = 56 * 1024 * 1024

N_A = 3 * W_GRP
N_B = 4 * W_GRP
N_D = 4 * W_GRP
N_C = 4 * W_GRP + 2 * LORA_PAD
OFF_A, OFF_B, OFF_D, OFF_C = 0, N_A, N_A + N_B, N_A + N_B + N_D
N_ALL = OFF_C + N_C


def _silu(g):
    return g * (1.0 / (1.0 + jnp.exp(-g)))


def _sigmoid(x):
    return 1.0 / (1.0 + jnp.exp(-x))


def _split_bf16(x):
    hi = x.astype(BF16)
    lo = (x - hi.astype(F32)).astype(BF16)
    return hi, lo


def _mm(a, b):
    return jnp.dot(a, b, preferred_element_type=F32)


def _mm_nt(a, b):
    return lax.dot_general(a, b, (((1,), (1,)), ((), ())), preferred_element_type=F32)


def _params(sem):
    return pltpu.CompilerParams(dimension_semantics=sem, vmem_limit_bytes=VMEM_LIMIT)


def _mod_kernel(s_ref, w_ref, b_ref, o_ref):
    s = _silu(s_ref[...])
    o_ref[0] = _mm(s.astype(BF16), w_ref[0].astype(BF16)) + b_ref[0]


def _modulation(c, c_ctx, w_mod, b_mod):
    depth, d, n3 = w_mod.shape
    b = c.shape[0]
    rows = jnp.zeros((8, d), F32).at[:b].set(c).at[b].set(c_ctx)
    tn = 768
    return pl.pallas_call(
        _mod_kernel,
        grid=(depth, n3 // tn),
        in_specs=[pl.BlockSpec((8, d), lambda l, j: (0, 0)),
                  pl.BlockSpec((1, d, tn), lambda l, j: (l, 0, j)),
                  pl.BlockSpec((1, 1, tn), lambda l, j: (l, 0, j))],
        out_specs=pl.BlockSpec((1, 8, tn), lambda l, j: (l, 0, j)),
        out_shape=jax.ShapeDtypeStruct((depth, 8, n3), F32),
        compiler_params=_params(("parallel", "parallel")),
        name="modulation",
    )(rows, w_mod, b_mod.reshape(depth, 1, n3))


def _inproj_kernel(n_batch, xc_ref, xl_ref, mod_ref, g_ref, w_ref, cos_ref, sa_ref, sb_ref,
                   pa_ref, pb_ref, pd_ref, pc_ref):
    b = pl.program_id(0)
    i = pl.program_id(1)
    d = xc_ref.shape[-1]
    is_ctx = i == 0
    x = jnp.where(is_ctx, xc_ref[0], xl_ref[0])
    y = x * lax.rsqrt(jnp.mean(x * x, axis=-1, keepdims=True) + EPS) * g_ref[...]
    row = jnp.where(is_ctx, mod_ref[0, pl.ds(n_batch, 1), :], mod_ref[0, pl.ds(b, 1), :])
    h = (y * (1.0 + row[:, d:2 * d]) + row[:, :d]).astype(BF16)

    def proj(c0, c1):
        return _mm(h, w_ref[0, :, c0:c1])

    pa_ref[0] = proj(OFF_A, OFF_A + N_A).astype(BF16)
    cos, sa, sb = cos_ref[...], sa_ref[...], sb_ref[...]
    qk = proj(OFF_B, OFF_B + 2 * W_GRP)
    for j in range(2 * W_GRP // HD_B):
        t = qk[:, j * HD_B:(j + 1) * HD_B]
        t = t * cos + pltpu.roll(t, HD_B - 16, axis=1) * sa + pltpu.roll(t, 16, axis=1) * sb
        if j < W_GRP // HD_B:
            t = t * (HD_QK ** -0.5 * LOG2E)
        pb_ref[0, :, j * HD_B:(j + 1) * HD_B] = t.astype(BF16)
    pb_ref[0, :, 2 * W_GRP:] = proj(OFF_B + 2 * W_GRP, OFF_B + N_B).astype(BF16)
    pd_ref[0] = proj(OFF_D, OFF_D + N_D).astype(BF16)
    pc_ref[0] = proj(OFF_C, OFF_C + N_C).astype(BF16)


def _inproj(xc, xl, mods, layer, g_pre, w_in, rope):
    b, n_ctx, d = xc.shape
    n_lat = xl.shape[1]
    t = n_ctx + n_lat
    assert n_ctx == TM and n_lat % TM == 0
    nblk = t // TM
    cos, sa, sb = rope
    tab = pl.BlockSpec((TM, HD_B), lambda bb, i: (i, 0))
    out = lambda n: pl.BlockSpec((1, TM, n), lambda bb, i: (bb, i, 0))
    return pl.pallas_call(
        functools.partial(_inproj_kernel, b),
        grid=(b, nblk),
        in_specs=[pl.BlockSpec((1, TM, d), lambda bb, i: (bb, 0, 0)),
                  pl.BlockSpec((1, TM, d), lambda bb, i: (bb, jnp.maximum(i - 1, 0), 0)),
                  pl.BlockSpec((1, 8, 3 * d), lambda bb, i: (layer, 0, 0)),
                  pl.BlockSpec((1, d), lambda bb, i: (0, 0)),
                  pl.BlockSpec((1, d, N_ALL), lambda bb, i: (layer, 0, 0), pipeline_mode=pl.Buffered(1)),
                  tab, tab, tab],
        out_specs=[out(N_A), out(N_B), out(N_D), out(N_C)],
        out_shape=[jax.ShapeDtypeStruct((b, t, n), BF16) for n in (N_A, N_B, N_D, N_C)],
        compiler_params=_params(("parallel", "arbitrary")),
        name="inproj",
    )(xc, xl, mods, g_pre.reshape(1, d), w_in, cos, sa, sb)


def _attn_kernel(lam_init, nsub, *refs):
    q_refs, (k_ref, v_ref), g_refs = refs[:nsub], refs[nsub:nsub + 2], refs[nsub + 2:2 * nsub + 2]
    lam_ref, sg_ref, o_ref, vext_ref = refs[2 * nsub + 2:]
    tq = nsub * TM
    n_keys = k_ref.shape[1]

    @pl.when(pl.program_id(2) == 0)
    def _():
        vext_ref[:, :HD_B] = v_ref[0]
        vext_ref[:, HD_B:] = jnp.ones((n_keys, HD_B), BF16)

    lp = lam_ref[...]
    lam = (jnp.exp(jnp.sum(lp[0:1] * lp[1:2], axis=-1, keepdims=True))
           - jnp.exp(jnp.sum(lp[2:3] * lp[3:4], axis=-1, keepdims=True)) + lam_init)
    lane = lax.broadcasted_iota(jnp.int32, (1, HD_B), 1)
    n_chain = -(-nsub // ATT_CHAIN)
    tc = tq // n_chain
    q2s = []
    for ci in range(n_chain):
        q = jnp.concatenate([r[0] for r in q_refs[ci * ATT_CHAIN:(ci + 1) * ATT_CHAIN]], axis=0)
        zero = jnp.zeros_like(q)
        q2s.append(jnp.concatenate([jnp.where(lane < HD_QK, q, zero),
                                    jnp.where(lane >= HD_QK, q, zero)], axis=0))
    ms = [None] * n_chain
    os_ = [None] * n_chain
    for k0 in range(0, n_keys, ATT_KB):
        k1 = min(k0 + ATT_KB, n_keys)
        for ci in range(n_chain):
            s = _mm_nt(q2s[ci], k_ref[0, k0:k1])
            bm = jnp.max(s, axis=-1, keepdims=True)
            if ms[ci] is None:
                ms[ci] = bm
                os_[ci] = _mm(jnp.exp2(s - bm).astype(BF16), vext_ref[k0:k1])
            else:
                m_new = jnp.maximum(ms[ci], bm)
                os_[ci] = (os_[ci] * jnp.exp2(ms[ci] - m_new)
                           + _mm(jnp.exp2(s - m_new).astype(BF16), vext_ref[k0:k1]))
                ms[ci] = m_new
    for ci in range(n_chain):
        o = os_[ci]
        a = o[:tc, :HD_B] / o[:tc, HD_B:] - lam * (o[tc:, :HD_B] / o[tc:, HD_B:])
        y = a * lax.rsqrt(jnp.mean(a * a, axis=-1, keepdims=True) + EPS) * sg_ref[...] * (1.0 - lam_init)
        g = jnp.concatenate([r[0] for r in g_refs[ci * ATT_CHAIN:(ci + 1) * ATT_CHAIN]], axis=0)
        o_ref[0, ci * tc:(ci + 1) * tc, :] = (y * _silu(g.astype(F32))).astype(BF16)


def _attention(pb, lam_params, subln_g, lam_init, q_blk0, n_q, n_keys, nsub):
    b = pb.shape[0]
    nh = W_GRP // HD_B
    assert n_q % (nsub * TM) == 0
    q_spec = lambda col0, j: pl.BlockSpec(
        (1, TM, HD_B), lambda bb, h, i: (bb, nsub * i + q_blk0 + j, col0 + h))
    kv_spec = lambda col0: pl.BlockSpec((1, n_keys, HD_B), lambda bb, h, i: (bb, 0, col0 + h))
    return pl.pallas_call(
        functools.partial(_attn_kernel, lam_init, nsub),
        grid=(b, nh, n_q // (nsub * TM)),
        in_specs=([q_spec(0, j) for j in range(nsub)] + [kv_spec(nh), kv_spec(2 * nh)]
                  + [q_spec(3 * nh, j) for j in range(nsub)]
                  + [pl.BlockSpec((4, HD_QK), lambda bb, h, i: (0, 0)),
                     pl.BlockSpec((1, HD_B), lambda bb, h, i: (0, 0))]),
        out_specs=pl.BlockSpec((1, nsub * TM, HD_B), lambda bb, h, i: (bb, i, h)),
        out_shape=jax.ShapeDtypeStruct((b, n_q, W_GRP), BF16),
        scratch_shapes=[pltpu.VMEM((n_keys, 2 * HD_B), BF16)],
        compiler_params=_params(("parallel", "parallel", "arbitrary")),
        name="diff_attention",
    )(*([pb] * (2 * nsub + 2)), lam_params, subln_g.reshape(1, HD_B))


def _seg_sum(x, ones_bd):
    hi, lo = _split_bf16(x)
    return _mm(hi, ones_bd) + _mm(lo, ones_bd)


def _head_diag(x4, mask):
    return jnp.where(mask, x4, 0.0).astype(BF16)


def _rwkv_prep_kernel(n_blk, *refs):
    dr = pl.program_id(1)
    pl.when(dr == 0)(lambda: _rwkv_prep_body(n_blk, True, *refs))
    pl.when(dr == 1)(lambda: _rwkv_prep_body(n_blk, False, *refs))


def _rwkv_prep_body(n_blk, fwd, rkv_ref, lo_ref, hrkv_ref, hlo_ref, mu_ref, mul_ref, w0_ref, w2_ref,
                    a0_ref, a2_ref, kk_ref, ka_ref, rk_ref,
                    g_ref, q_ref, rp_ref, yp_ref, bonus_ref):
    i = pl.program_id(2)
    nl = RW_G // HD_C
    sgn = 1 if fwd else -1
    hrow = 15 if fwd else 0
    edge = jnp.logical_or(i == 0, i == (1 if fwd else n_blk - 1))
    rsel = lax.broadcasted_iota(jnp.int32, (TM, 1), 0) == (0 if fwd else TM - 1)

    def shifted(cur_ref, halo_ref, mu):
        f = cur_ref[0].astype(F32)
        halo = jnp.where(edge, 0.0, halo_ref[0, hrow:hrow + 1, :].astype(F32))
        prev = jnp.where(rsel, halo, pltpu.roll(f, 1 if fwd else TM - 1, axis=0))
        return f + (prev - f) * mu

    z = shifted(rkv_ref, hrkv_ref, mu_ref[0])
    zl = shifted(lo_ref, hlo_ref, mul_ref[0])
    r, k, v = z[:, :W_GRP], z[:, W_GRP:2 * W_GRP], z[:, 2 * W_GRP:]
    wl = w0_ref[0] + _mm(jnp.tanh(zl).astype(BF16), w2_ref[0].astype(BF16))
    logd = -math.exp(-0.5) * _sigmoid(wl)
    a = _sigmoid(a0_ref[0] + _mm(zl.astype(BF16), a2_ref[0].astype(BF16)))
    ri = lax.broadcasted_iota(jnp.int32, (W_GRP, W_GRP), 0) // HD_C
    ci = lax.broadcasted_iota(jnp.int32, (W_GRP, W_GRP), 1) // HD_C
    ones_bd = jnp.where(ri == ci, 1.0, 0.0).astype(BF16)
    kk = k * kk_ref[0]
    kk = kk / jnp.maximum(jnp.sqrt(_seg_sum(kk * kk, ones_bd)), 1e-12)
    k2 = k * (1.0 + (a - 1.0) * ka_ref[0])
    bv = kk * a
    bonus_ref[0, 0] = (_seg_sum(r * k2 * rk_ref[0], ones_bd) * v).astype(BF16)
    rt = lax.broadcasted_iota(jnp.int32, (TM, TM), 0)
    ct = lax.broadcasted_iota(jnp.int32, (TM, TM), 1)
    same = rt // RW_L == ct // RW_L
    tri = jnp.where(jnp.logical_and(same, (rt - ct) * sgn >= 0), 1.0, 0.0).astype(BF16)
    blk = jnp.where(same, 1.0, 0.0).astype(BF16)
    lh, ll = _split_bf16(logd)
    cum = _mm(tri, lh) + _mm(tri, ll)
    tot = _mm(blk, lh) + _mm(blk, ll)
    half = 0.5 * tot
    e_pos = jnp.exp(cum - half)
    e_neg = jnp.exp(half - cum)
    e_prev = jnp.exp(cum - logd - half)
    e_half = jnp.exp(half)
    e_end = e_half * e_neg
    g_end = e_half * e_half
    kt_all = kk * e_prev
    rt_all = r * e_pos
    kh_all = k2 * e_neg
    bh_all = bv * e_neg
    kb_all = k2 * e_end
    bb_all = bv * e_end
    row = lax.broadcasted_iota(jnp.int32, (RW_L, RW_G), 0)
    col = lax.broadcasted_iota(jnp.int32, (RW_L, RW_G), 1) % HD_C
    eye = jnp.where(col == row, 1.0, 0.0)
    r4 = lax.broadcasted_iota(jnp.int32, (RW_G, RW_G), 0) // HD_C
    c4 = lax.broadcasted_iota(jnp.int32, (RW_G, RW_G), 1) // HD_C
    dmask = r4 == c4
    r2 = lax.broadcasted_iota(jnp.int32, (2 * RW_L, RW_G), 0) % RW_L
    c2 = lax.broadcasted_iota(jnp.int32, (2 * RW_L, RW_G), 1) % HD_C
    first = jnp.where(lax.broadcasted_iota(jnp.int32, (2 * RW_L, RW_G), 0) < RW_L, 1, 0)
    mask2 = (r2 - c2) * sgn >= first

    def bd(x):
        return _head_diag(jnp.concatenate([x] * nl, axis=0), dmask)

    def hprod(x, y):
        return _mm(x.astype(BF16), bd(y))

    def diag_blocks(full):
        out = jnp.zeros((RW_L, RW_G), F32)
        lane_h = lax.broadcasted_iota(jnp.int32, (RW_L, RW_G), 1) // HD_C
        for hh in range(nl):
            out = out + jnp.where(lane_h == hh, full[hh * HD_C:(hh + 1) * HD_C, :], 0.0)
        return out

    pairs = [(slice(c * RW_L, (c + 1) * RW_L), slice(gi * RW_G, (gi + 1) * RW_G))
             for c in range(TM // RW_L) for gi in range(W_GRP // RW_G)]
    cat = lambda *xs: jnp.concatenate(xs, axis=0)
    L = RW_L
    kt = [kt_all[p] for p in pairs]
    rtl = [rt_all[p] for p in pairs]
    vv = [v[p] for p in pairs]
    lhs = [cat(a_, b_).astype(BF16) for a_, b_ in zip(kt, rtl)]
    pb_ = [jnp.where(mask2, _mm_nt(x, bd(bh_all[p])), 0.0) for x, p in zip(lhs, pairs)]
    pk_ = [jnp.where(mask2, _mm_nt(x, bd(kh_all[p])), 0.0) for x, p in zip(lhs, pairs)]
    res = [hprod(x, x[:L]) for x in pb_]
    tinv = [eye - x[:L] for x in pb_]
    mt = [x[L:] - y[L:] for x, y in zip(pb_, res)]
    pw = [y[:L] for y in res]
    for _ in range(4):
        res = [hprod(cat(t_, m_, w_), w_) for t_, m_, w_ in zip(tinv, mt, pw)]
        tinv = [t_ + y[:L] for t_, y in zip(tinv, res)]
        mt = [m_ + y[L:2 * L] for m_, y in zip(mt, res)]
        pw = [y[2 * L:] for y in res]
    res = [hprod(cat(t_, m_), w_) for t_, m_, w_ in zip(tinv, mt, pw)]
    tm = [cat(t_, m_) + y for t_, m_, y in zip(tinv, mt, res)]
    wv = [hprod(x, y) for x, y in zip(pk_, vv)]
    tmb = [x.astype(BF16) for x in tm]
    ww = [_mm(x, bd(y)) for x, y in zip(tmb, kt)]
    uu = [_mm(x, bd(y[:L])) for x, y in zip(tmb, wv)]
    for j, p in enumerate(pairs):
        rows, cols = p
        eh = e_half[rows.start:rows.start + 1, cols]
        rp_ref[0, 0, rows, cols] = ((rtl[j] - ww[j][L:]) * eh).astype(BF16)
        yp_ref[0, 0, rows, cols] = (wv[j][L:] - uu[j][L:]).astype(BF16)
        xt = cat(kb_all[p], -bb_all[p]).T.astype(BF16)
        zeros = jnp.zeros((L, RW_G), F32)
        yq = cat(vv[j], uu[j][:L]).astype(BF16)
        yg = cat(zeros, ww[j][:L]).astype(BF16)
        q_ref[0, 0, rows, cols] = diag_blocks(_mm(xt, yq))
        g_ref[0, 0, rows, cols] = (diag_blocks(_mm(xt, yg)) * eh
                                   + eye * g_end[rows.start:rows.start + 1, cols])


def _rwkv_scan_kernel(n_chain, *refs):
    ins, y_refs, h_ref = refs[:4 * n_chain], refs[4 * n_chain:5 * n_chain], refs[-1]
    nl = RW_G // HD_C

    @pl.when(pl.program_id(0) == 0)
    def _():
        h_ref[...] = jnp.zeros_like(h_ref)

    r4 = lax.broadcasted_iota(jnp.int32, (RW_G, RW_G), 0) // HD_C
    c4 = lax.broadcasted_iota(jnp.int32, (RW_G, RW_G), 1) // HD_C
    dmask = r4 == c4
    n_chunk = TM // RW_L
    links = [(j, slice(gi * RW_G, (gi + 1) * RW_G)) for j in range(n_chain)
             for gi in range(W_GRP // RW_G)]
    h = [h_ref[j, :, cols] for j, cols in links]
    for s in range(n_chunk):
        prods = []
        for (j, cols), hj in zip(links, h):
            g_ref, _, rp_ref, _ = ins[4 * j:4 * j + 4]
            c = n_chunk - 1 - s if j % 2 else s
            rows = slice(c * RW_L, (c + 1) * RW_L)
            h4 = jnp.where(dmask, jnp.concatenate([hj] * nl, axis=0), 0.0)
            hh, hl = _split_bf16(h4)
            gh, gl = _split_bf16(g_ref[0, 0, rows, cols])
            lh = jnp.concatenate([gh, rp_ref[0, 0, rows, cols]], axis=0)
            prods.append((_mm(lh, hh) + _mm(lh, hl), _mm(gl, hh)))
        for k, ((j, cols), (prod, corr)) in enumerate(zip(links, prods)):
            _, q_ref, _, yp_ref = ins[4 * j:4 * j + 4]
            c = n_chunk - 1 - s if j % 2 else s
            rows = slice(c * RW_L, (c + 1) * RW_L)
            y_refs[j][0, 0, rows, cols] = (prod[RW_L:] + yp_ref[0, 0, rows, cols].astype(F32)).astype(BF16)
            h[k] = prod[:RW_L] + corr + q_ref[0, 0, rows, cols]
    for (j, cols), hj in zip(links, h):
        h_ref[j, :, cols] = hj


def _scan_block(dr, i, n_blk):
    return jnp.where(dr == 0, i, jnp.where(i == 0, 0, n_blk - i))


def _rwkv(pc, p):
    b, t, _ = pc.shape
    nblk = t // TM
    hb = TM // 16
    lblk = 4 * W_GRP // LORA_PAD

    def halo_idx(dr, i):
        return jnp.where(dr == 0, jnp.maximum(i * hb - 1, 0), jnp.minimum((i + 1) * hb, nblk * hb - 1))

    per_dir = lambda n: pl.BlockSpec((1, 1, n), lambda bb, dr, i: (dr, 0, 0))
    per_dir_m = lambda m, n: pl.BlockSpec((1, m, n), lambda bb, dr, i: (dr, 0, 0))
    state = lambda: pl.BlockSpec((1, 1, TM, W_GRP), lambda bb, dr, i: (bb, dr, i, 0))
    sds = lambda dt: jax.ShapeDtypeStruct((b, 2, t, W_GRP), dt)
    gm, qm, rp, yp, bonus = pl.pallas_call(
        functools.partial(_rwkv_prep_kernel, nblk),
        grid=(b, 2, nblk),
        in_specs=[pl.BlockSpec((1, TM, 3 * W_GRP), lambda bb, dr, i: (bb, i, 0)),
                  pl.BlockSpec((1, TM, LORA_PAD), lambda bb, dr, i: (bb, i, lblk + dr)),
                  pl.BlockSpec((1, 16, 3 * W_GRP), lambda bb, dr, i: (bb, halo_idx(dr, i), 0)),
                  pl.BlockSpec((1, 16, LORA_PAD), lambda bb, dr, i: (bb, halo_idx(dr, i), lblk + dr)),
                  per_dir(3 * W_GRP), per_dir(LORA_PAD), per_dir(W_GRP), per_dir_m(LORA_PAD, W_GRP),
                  per_dir(W_GRP), per_dir_m(LORA_PAD, W_GRP), per_dir(W_GRP), per_dir(W_GRP),
                  per_dir(W_GRP)],
        out_specs=[state()] * 5,
        out_shape=[sds(F32), sds(F32), sds(BF16), sds(BF16), sds(BF16)],
        compiler_params=_params(("parallel", "parallel", "parallel")),
        name="rwkv_prep",
    )(pc, pc, pc, pc, p["mu_rkv"], p["mu_lora"], p["w0"], p["w2"], p["a0"], p["a2"],
      p["kk"], p["ka"], p["rk"])

    chains = [(bb, dr) for bb in range(b) for dr in range(2)]
    scan_spec = lambda bb, dr: pl.BlockSpec(
        (1, 1, TM, W_GRP), lambda i: (bb, dr, _scan_block(dr, i, nblk), 0))
    sds1 = jax.ShapeDtypeStruct((1, 1, t, W_GRP), BF16)
    ys = pl.pallas_call(
        functools.partial(_rwkv_scan_kernel, len(chains)),
        grid=(nblk,),
        in_specs=[scan_spec(bb, dr) for bb, dr in chains for _ in range(4)],
        out_specs=[pl.BlockSpec((1, 1, TM, W_GRP), lambda i, dr=dr: (0, 0, _scan_block(dr, i, nblk), 0))
                   for _, dr in chains],
        out_shape=[sds1] * len(chains),
        scratch_shapes=[pltpu.VMEM((len(chains), RW_L, W_GRP), F32)],
        compiler_params=_params(("arbitrary",)),
        name="rwkv_scan",
    )(*[a for _ in chains for a in (gm, qm, rp, yp)])
    return ys, bonus


def _sgu(pa_ref, bb, ws_ref, bs_ref):
    rows_out = []
    for c in range(TM // CHUNK_A):
        rows = slice(c * CHUNK_A, (c + 1) * CHUNK_A)
        heads = []
        for h in range(W_GRP // CHUNK_A):
            cols = lambda k: slice(k * W_GRP + h * CHUNK_A, k * W_GRP + (h + 1) * CHUNK_A)
            u = pa_ref[bb, rows, cols(0)].astype(F32)
            v = pa_ref[bb, rows, cols(1)].astype(F32)
            g = pa_ref[bb, rows, cols(2)].astype(F32)
            vc = v - jnp.mean(v, axis=-1, keepdims=True)
            vn = vc * lax.rsqrt(jnp.mean(vc * vc, axis=-1, keepdims=True) + EPS)
            mixed = _mm(ws_ref[h].astype(BF16), vn.astype(BF16)) + bs_ref[:, h:h + 1]
            heads.append((u * mixed * _silu(g)).astype(BF16))
        rows_out.append(jnp.concatenate(heads, axis=1))
    return jnp.concatenate(rows_out, axis=0)


def _short_conv(pd_ref, hp_ref, hn_ref, bb, cw_ref, first, last):
    cx = pd_ref[bb, :, 0:2 * W_GRP].astype(F32)
    z = cx[:, :W_GRP] * cx[:, W_GRP:]
    hp = hp_ref[bb].astype(F32)
    hn = hn_ref[bb].astype(F32)
    zp = jnp.where(first, 0.0, hp[15:16, :W_GRP] * hp[15:16, W_GRP:])
    zn = jnp.where(last, 0.0, hn[0:1, :W_GRP] * hn[0:1, W_GRP:])
    r = lax.broadcasted_iota(jnp.int32, (TM, 1), 0)
    z_prev = jnp.where(r == 0, zp, pltpu.roll(z, 1, axis=0))
    z_next = jnp.where(r == TM - 1, zn, pltpu.roll(z, TM - 1, axis=0))
    w = cw_ref[...]
    y = z_prev * w[0:1] + z * w[1:2] + z_next * w[2:3]
    bg = pd_ref[bb, :, 2 * W_GRP:3 * W_GRP].astype(F32)
    g = pd_ref[bb, :, 3 * W_GRP:].astype(F32)
    return (bg * y * _silu(g)).astype(BF16)


def _outproj_kernel(mod_rows, blk0, n_blk, *refs):
    n_batch = len(mod_rows)
    pa_ref, pd_ref, hp_ref, hn_ref, ws_ref, bs_ref, cw_ref, ob_ref = refs[:8]
    y_refs = refs[8:8 + 2 * n_batch]
    bonus_ref, gc_ref, lw_ref, lb_ref, w_ref, x_ref, mod_ref, g_ref, o_ref = refs[8 + 2 * n_batch:]
    d = x_ref.shape[-1]
    ib = pl.program_id(0) + blk0
    first = ib <= 1
    last = jnp.logical_or(ib == 0, ib == n_blk - 1)
    ri = lax.broadcasted_iota(jnp.int32, (W_GRP, W_GRP), 0) // HD_C
    ci = lax.broadcasted_iota(jnp.int32, (W_GRP, W_GRP), 1) // HD_C
    ones_bd = jnp.where(ri == ci, 1.0, 0.0).astype(BF16)
    for bb in range(n_batch):
        oa = _sgu(pa_ref, bb, ws_ref, bs_ref)
        od = _short_conv(pd_ref, hp_ref, hn_ref, bb, cw_ref, first, last)
        yc = y_refs[2 * bb][0, 0].astype(F32) + y_refs[2 * bb + 1][0, 0].astype(F32)
        yc = yc - _mm(yc.astype(BF16), ones_bd) * (1.0 / HD_C)
        var = _mm((yc * yc).astype(BF16), ones_bd) * (1.0 / HD_C)
        yn = yc * lax.rsqrt(var + GN_EPS) * lw_ref[...] + lb_ref[...]
        oc = yn + bonus_ref[bb, 0].astype(F32) + bonus_ref[bb, 1].astype(F32)
        oc = (oc * _silu(gc_ref[bb].astype(F32))).astype(BF16)
        mix = jnp.concatenate([oa, ob_ref[bb], oc, od], axis=1)
        acc = _mm(mix, w_ref[0])
        y = acc * lax.rsqrt(jnp.mean(acc * acc, axis=-1, keepdims=True) + EPS) * g_ref[...]
        o_ref[bb] = x_ref[bb] + mod_ref[0, mod_rows[bb]:mod_rows[bb] + 1, 2 * d:] * y


def _outproj(pa, pd, sgu_w, sgu_b, conv_w, o_b, ys, bonus, pc, ln_w, ln_b, w_out, x, mods, layer,
             g_post, blk0, ob_blk0, mod_rows):
    b, n, d = x.shape
    t = pa.shape[1]
    nblk = t // TM
    hb = TM // 16
    tok = lambda off, col, width: pl.BlockSpec((b, TM, width), lambda i: (0, i + off, col))
    return pl.pallas_call(
        functools.partial(_outproj_kernel, mod_rows, blk0, nblk),
        grid=(n // TM,),
        in_specs=[tok(blk0, 0, N_A), tok(blk0, 0, N_D),
                  pl.BlockSpec((b, 16, 2 * W_GRP),
                               lambda i: (0, jnp.maximum((i + blk0) * hb - 1, 0), 0)),
                  pl.BlockSpec((b, 16, 2 * W_GRP),
                               lambda i: (0, jnp.minimum((i + blk0 + 1) * hb, nblk * hb - 1), 0)),
                  pl.BlockSpec(sgu_w.shape, lambda i: (0, 0, 0)),
                  pl.BlockSpec((CHUNK_A, sgu_w.shape[0]), lambda i: (0, 0)),
                  pl.BlockSpec(conv_w.shape, lambda i: (0, 0)),
                  tok(ob_blk0, 0, W_GRP)]
        + [pl.BlockSpec((1, 1, TM, W_GRP), lambda i: (0, 0, i + blk0, 0))] * len(ys)
        + [pl.BlockSpec((b, 2, TM, W_GRP), lambda i: (0, 0, i + blk0, 0)),
           tok(blk0, 3, W_GRP),
           pl.BlockSpec((1, W_GRP), lambda i: (0, 0)),
           pl.BlockSpec((1, W_GRP), lambda i: (0, 0)),
           pl.BlockSpec((1,) + w_out.shape[1:], lambda i: (layer, 0, 0)),
           pl.BlockSpec((b, TM, d), lambda i: (0, i, 0)),
           pl.BlockSpec((1, 8, 3 * d), lambda i: (layer, 0, 0)),
           pl.BlockSpec((1, d), lambda i: (0, 0))],
        out_specs=pl.BlockSpec((b, TM, d), lambda i: (0, i, 0)),
        out_shape=jax.ShapeDtypeStruct((b, n, d), F32),
        compiler_params=_params(("parallel",)),
        name="outproj",
    )(pa, pd, pd, pd, sgu_w, sgu_b.T, conv_w, o_b, *ys, bonus, pc, ln_w, ln_b, w_out, x, mods,
      g_post.reshape(1, d))


def _w_in_segments():
    g = W_GRP
    o = [0]
    for s in (g, g, g, g, g, g, g, 3 * g, LORA_W + LORA_A, LORA_W + LORA_A, g, g, g, g, g):
        o.append(o[-1] + s)
    order = [0, 1, 2, 3, 4, 5, 6, 12, 13, 11, 14, 7, 10, 8, 9]
    segs, dst = [], 0
    for j in order:
        width = o[j + 1] - o[j]
        segs.append((o[j], width, dst))
        dst += -(-width // LORA_PAD) * LORA_PAD
    assert dst == N_ALL
    return segs, o[-1]


def _w_in_kernel(wt_ref, o_ref):
    segs, _ = _w_in_segments()
    cols = wt_ref.shape[2]
    for src, width, dst in segs:
        piece = wt_ref[0, src:src + width, :]
        padded = -(-width // LORA_PAD) * LORA_PAD
        if padded != width:
            piece = jnp.concatenate([piece, jnp.zeros((padded - width, cols), F32)], axis=0)
        o_ref[0, :, dst:dst + padded] = piece.T.astype(BF16)


def _regroup_w_in(w_in):
    depth, d, n_in = w_in.shape
    assert n_in == _w_in_segments()[1]
    tr = 256
    return pl.pallas_call(
        _w_in_kernel,
        grid=(depth, d // tr),
        in_specs=[pl.BlockSpec((1, n_in, tr), lambda l, i: (l, 0, i))],
        out_specs=pl.BlockSpec((1, tr, N_ALL), lambda l, i: (l, i, 0)),
        out_shape=jax.ShapeDtypeStruct((depth, d, N_ALL), BF16),
        compiler_params=_params(("parallel", "parallel")),
        name="w_in_prep",
    )(jnp.swapaxes(w_in, 1, 2))


def _cast_kernel(w_ref, o_ref):
    o_ref[...] = w_ref[...].astype(BF16)


def _cast_w_out(w_out):
    depth, k, d = w_out.shape
    tr = 512
    return pl.pallas_call(
        _cast_kernel,
        grid=(depth, k // tr),
        in_specs=[pl.BlockSpec((1, tr, d), lambda l, i: (l, i, 0))],
        out_specs=pl.BlockSpec((1, tr, d), lambda l, i: (l, i, 0)),
        out_shape=jax.ShapeDtypeStruct((depth, k, d), BF16),
        compiler_params=_params(("parallel", "parallel")),
        name="w_out_prep",
    )(w_out)


def _rope_tables(n_ctx, n_lat):
    n_rows = n_lat // GRID_W
    inv = ROPE_BASE ** (-jnp.arange(0, 32, 2, dtype=F32) / 32)
    lane = jnp.arange(HD_B)
    use_col = ((lane // 32) % 2 == 1)[None, None, :]
    trig = lambda f, n: f(jnp.arange(n, dtype=F32)[:, None] * inv[lane % 16][None, :])
    grid = lambda f: jnp.where(use_col, trig(f, GRID_W)[None, :, :], trig(f, n_rows)[:, None, :]
                               ).reshape(n_lat, HD_B)
    first = (lane % 32 < 16)[None, :]
    cos, sin = grid(jnp.cos), grid(jnp.sin)
    sa = jnp.where(first, -sin, 0.0)
    sb = jnp.where(first, 0.0, sin)
    ident = lambda v: jnp.full((n_ctx, HD_B), v, F32)
    return (jnp.concatenate([ident(1.0), cos]), jnp.concatenate([ident(0.0), sa]),
            jnp.concatenate([ident(0.0), sb]))


def _rwkv_params(l, mu, w0, w2, a0, a2, kk, ka, rk, ln_w, ln_b):
    g = W_GRP
    zpad = lambda n: jnp.zeros((2, n), F32)
    mu_l = jnp.concatenate([mu[l][:, 3 * g:], zpad(LORA_PAD - LORA_W - LORA_A)], axis=1)
    w2p = jnp.concatenate([w2[l], jnp.zeros((2, LORA_PAD - LORA_W, g), F32)], axis=1)
    a2p = jnp.concatenate([jnp.zeros((2, LORA_W, g), F32), a2[l],
                           jnp.zeros((2, LORA_PAD - LORA_W - LORA_A, g), F32)], axis=1)
    r3 = lambda v: v.reshape(2, 1, -1)
    return dict(mu_rkv=r3(mu[l][:, :3 * g]), mu_lora=r3(mu_l), w0=r3(w0[l]), w2=w2p, a0=r3(a0[l]),
                a2=a2p, kk=r3(kk[l]), ka=r3(ka[l]), rk=r3(rk[l].reshape(2, g)),
                ln_w=ln_w[l].reshape(1, g), ln_b=ln_b[l].reshape(1, g))


def kernel(x, c, ctx, c_ctx, w_mod, b_mod, g_pre, g_post, w_in, w_out, sgu_w, sgu_b, lam_q1, lam_k1, lam_q2, lam_k2, subln_g, rwkv_mu, rwkv_w0, rwkv_w2, rwkv_a0, rwkv_a2, rwkv_kk, rwkv_ka, rwkv_rk, rwkv_ln_w, rwkv_ln_b, conv_w):
    depth = w_mod.shape[0]
    b, n_lat, _ = x.shape
    n_ctx = ctx.shape[1]
    mods = _modulation(c, c_ctx, w_mod, b_mod)
    rope = _rope_tables(n_ctx, n_lat)
    w_in_b = _regroup_w_in(w_in)
    w_out_b = _cast_w_out(w_out)
    xc = ctx
    for l in range(depth):
        need_ctx = l < depth - 1
        lam_init = 0.8 - 0.6 * math.exp(-0.3 * l)
        pa, pb, pd, pc = _inproj(xc, x, mods, l, g_pre[l], w_in_b, rope)
        lam_params = jnp.stack([lam_q1[l], lam_k1[l], lam_q2[l], lam_k2[l]])
        cb = n_ctx // TM
        o_b = _attention(pb, lam_params, subln_g[l], lam_init, cb, n_lat, n_ctx + n_lat, ATT_NSUB)
        rp = _rwkv_params(l, rwkv_mu, rwkv_w0, rwkv_w2, rwkv_a0, rwkv_a2, rwkv_kk, rwkv_ka, rwkv_rk,
                          rwkv_ln_w, rwkv_ln_b)
        ys, bonus = _rwkv(pc, rp)
        local = (pa, pd, sgu_w[l], sgu_b[l], conv_w[l])
        tail = (ys, bonus, pc, rp["ln_w"], rp["ln_b"], w_out_b)
        x_new = _outproj(*local, o_b, *tail, x, mods, l, g_post[l], cb, 0, tuple(range(b)))
        if need_ctx:
            o_bc = _attention(pb, lam_params, subln_g[l], lam_init, 0, n_ctx, n_ctx, 1)
            xc = _outproj(*local, o_bc, *tail, xc, mods, l, g_post[l], 0, 0, (b,) * b)
        x = x_new
    return x
```
